```python
import math
import jax, jax.numpy as jnp
from jax import lax
import numpy as np

D_MODEL = 1024
BATCH = 1
SEQ = 16384
DEPTH = 2
DEC_BATCH = 32
DEC_SEQ = 8
PAST_LEN = 16384
PAGE_SIZE = 128

N_HEADS = 8
HEAD_DIM = D_MODEL // N_HEADS
IDX_HEADS = 8
IDX_DIM = 64
TOPK_MAX = 256
CONV_WIDTH = 31
CONV_STATE = CONV_WIDTH - 1
D_FF = ((8 * D_MODEL + 3 * 256 - 1) // (3 * 256)) * 256
REL_BUCKETS = 32
REL_MAX_DIST = 128
Q_BLOCK = 128
N_CONV_LAYERS = (DEPTH + 1) // 2
N_ATTN_LAYERS = DEPTH // 2
ALPHA = (2 * DEPTH) ** 0.25
BETA = (8 * DEPTH) ** -0.25
LN_EPS = 1e-5
P_IN = 3 * D_MODEL + IDX_HEADS * IDX_DIM + IDX_DIM + IDX_HEADS

kernel_name = 'hybrid_conformerconv_dsa_decoder_step'


def layer_norm(x, g, b):
    xf = x.astype(jnp.float32)
    mu = jnp.mean(xf, axis=-1, keepdims=True)
    var = jnp.mean(jnp.square(xf - mu), axis=-1, keepdims=True)
    return ((xf - mu) * lax.rsqrt(var + LN_EPS) * g.astype(jnp.float32) + b.astype(jnp.float32)).astype(x.dtype)


def rel_bucket(dist):
    dist = jnp.maximum(dist, 0)
    max_exact = REL_BUCKETS // 2
    d = jnp.maximum(dist, max_exact).astype(jnp.float32)
    large = max_exact + (jnp.log(d / max_exact) / math.log(REL_MAX_DIST / max_exact)
                         * (REL_BUCKETS - max_exact)).astype(jnp.int32)
    large = jnp.minimum(large, REL_BUCKETS - 1)
    return jnp.where(dist < max_exact, dist, large)


def conv_module(x, buf, w_pw1, b_pw1, w_dw, b_dw, g_n, b_n, w_pw2, b_pw2):
    h = x @ w_pw1 + b_pw1
    a, gate = jnp.split(h, 2, axis=-1)
    u = a * jax.nn.sigmoid(gate)
    full = jnp.concatenate([buf.astype(u.dtype), u], axis=1)
    y = lax.conv_general_dilated(full, w_dw[:, None, :], window_strides=(1,), padding='VALID',
                                 dimension_numbers=('NWC', 'WIO', 'NWC'),
                                 feature_group_count=D_MODEL) + b_dw
    y = jax.nn.silu(layer_norm(y, g_n, b_n))
    return y @ w_pw2 + b_pw2, full[:, -CONV_STATE:]


def swiglu(x, w_in, w_out):
    g, u = jnp.split(x @ w_in, 2, axis=-1)
    return (jax.nn.silu(g) * u) @ w_out


def attn_project(x, w_in):
    B, T, _ = x.shape
    p = x @ w_in
    o1 = D_MODEL
    o2 = 2 * D_MODEL
    o3 = 3 * D_MODEL
    o4 = o3 + IDX_HEADS * IDX_DIM
    o5 = o4 + IDX_DIM
    q, k, v, qi, ki, wi = jnp.split(p, [o1, o2, o3, o4, o5], axis=-1)
    q = q.reshape(B, T, N_HEADS, HEAD_DIM)
    k = k.reshape(B, T, N_HEADS, HEAD_DIM)
    v = v.reshape(B, T, N_HEADS, HEAD_DIM)
    qi = qi.reshape(B, T, IDX_HEADS, IDX_DIM)
    wi = wi * IDX_HEADS ** -0.5
    return q, k, v, qi, ki, wi


def indexer_topk(qi, wi, ki, pos_q, pos_k, k_sel):
    s = jnp.einsum('bthd,bsd->bths', qi, ki, preferred_element_type=jnp.float32) * IDX_DIM ** -0.5
    score = jnp.einsum('bths,bth->bts', jax.nn.relu(s), wi.astype(jnp.float32))
    admissible = pos_k[None, :] <= pos_q[:, None]
    score = jnp.where(admissible[None], score, -jnp.inf)
    _, idx = lax.top_k(score, k_sel)
    return idx


def sparse_attend(q, kg, vg, pos_q, idx, rel_bias):
    dist = pos_q[None, :, None] - idx
    bias = jnp.moveaxis(rel_bias[rel_bucket(dist)], -1, -2)
    logits = jnp.einsum('bthd,btkhd->bthk', q, kg, preferred_element_type=jnp.float32) * HEAD_DIM ** -0.5
    logits = logits + bias.astype(jnp.float32)
    logits = jnp.where((dist >= 0)[:, :, None, :], logits, -jnp.inf)
    p = jax.nn.softmax(logits, axis=-1)
    return jnp.einsum('bthk,btkhd->bthd', p.astype(vg.dtype), vg)


def take_rows(rows, idx):
    return jax.vmap(lambda r, i: r[i])(rows, idx)


def attn_prompt(x, w_in, w_o, rel_bias):
    B, T, _ = x.shape
    q, k, v, qi, ki, wi = attn_project(x, w_in)
    k_sel = min(TOPK_MAX, T // 4)
    pos = jnp.arange(T, dtype=jnp.int32)
    nb = T // Q_BLOCK

    def block(args):
        qb, qib, wib, posb = args
        idx = indexer_topk(qib, wib, ki, posb, pos, k_sel)
        return sparse_attend(qb, take_rows(k, idx), take_rows(v, idx), posb, idx, rel_bias)

    to_blocks = lambda a: jnp.moveaxis(a.reshape((B, nb, Q_BLOCK) + a.shape[2:]), 1, 0)
    out = lax.map(block, (to_blocks(q), to_blocks(qi), to_blocks(wi), pos.reshape(nb, Q_BLOCK)))
    out = jnp.moveaxis(out, 0, 1).reshape(B, T, D_MODEL)
    return out @ w_o, k, v, ki


def attn_sample(x, cache_k, cache_v, cache_kidx, page_table, layer, w_in, w_o, rel_bias):
    B, T, _ = x.shape
    past = page_table.shape[1] * PAGE_SIZE
    q, k, v, qi, ki, wi = attn_project(x, w_in)
    L = past + T
    k_sel = min(TOPK_MAX, L // 4)
    pos_q = past + jnp.arange(T, dtype=jnp.int32)
    pos_k = jnp.arange(L, dtype=jnp.int32)
    ki_past = cache_kidx[layer, page_table].reshape(B, past, IDX_DIM)
    ki_all = jnp.concatenate([ki_past.astype(ki.dtype), ki], axis=1)
    idx = indexer_topk(qi, wi, ki_all, pos_q, pos_k, k_sel)
    in_past = idx < past
    pidx = jnp.minimum(idx, past - 1)
    phys = jnp.take_along_axis(page_table, (pidx // PAGE_SIZE).reshape(B, -1), axis=1).reshape(idx.shape)
    off = pidx % PAGE_SIZE
    nidx = jnp.clip(idx - past, 0, T - 1)
    sel = in_past[..., None, None]
    kg = jnp.where(sel, cache_k[layer, phys, off].astype(k.dtype), take_rows(k, nidx))
    vg = jnp.where(sel, cache_v[layer, phys, off].astype(v.dtype), take_rows(v, nidx))
    out = sparse_attend(q, kg, vg, pos_q, idx, rel_bias).reshape(B, T, D_MODEL)
    return out @ w_o, k, v, ki


def setup_inputs(seed: int = 0) -> dict:
    key = jax.random.key(seed)
    ks = jax.random.split(key, 32)
    nrm = lambda k, shape, scale: jax.random.normal(k, shape, jnp.float32) * scale
    n_pages = PAST_LEN // PAGE_SIZE
    used = DEC_BATCH * n_pages
    n_pool = used + max(1, used // 4)
    page_table = jax.random.permutation(ks[0], n_pool)[:used].astype(jnp.int32).reshape(DEC_BATCH, n_pages)
    NC, NA = N_CONV_LAYERS, N_ATTN_LAYERS
    return {
        'x_prompt': nrm(ks[1], (BATCH, SEQ, D_MODEL), 1.0),
        'x_sample': nrm(ks[2], (DEC_BATCH, DEC_SEQ, D_MODEL), 1.0),
        'state_conv': nrm(ks[3], (NC, DEC_BATCH, CONV_STATE, D_MODEL), 0.5),
        'cache_k': nrm(ks[4], (NA, n_pool, PAGE_SIZE, N_HEADS, HEAD_DIM), 1.0),
        'cache_v': nrm(ks[5], (NA, n_pool, PAGE_SIZE, N_HEADS, HEAD_DIM), 1.0),
        'cache_kidx': nrm(ks[6], (NA, n_pool, PAGE_SIZE, IDX_DIM), 1.0),
        'page_table': page_table,
        'rel_bias': nrm(ks[7], (REL_BUCKETS, N_HEADS), 0.1),
        'w_pw1': nrm(ks[8], (NC, D_MODEL, 2 * D_MODEL), D_MODEL ** -0.5),
        'b_pw1': nrm(ks[9], (NC, 2 * D_MODEL), 0.02),
        'w_dw': nrm(ks[10], (NC, CONV_WIDTH, D_MODEL), CONV_WIDTH ** -0.5),
        'b_dw': nrm(ks[11], (NC, D_MODEL), 0.02),
        'conv_norm_g': 1.0 + nrm(ks[12], (NC, D_MODEL), 0.02),
        'conv_norm_b': nrm(ks[13], (NC, D_MODEL), 0.02),
        'w_pw2': nrm(ks[14], (NC, D_MODEL, D_MODEL), BETA * D_MODEL ** -0.5),
        'b_pw2': nrm(ks[15], (NC, D_MODEL), 0.02),
        'w_attn_in': nrm(ks[16], (NA, D_MODEL, P_IN), D_MODEL ** -0.5),
        'w_attn_out': nrm(ks[17], (NA, D_MODEL, D_MODEL), BETA * D_MODEL ** -0.5),
        'w_ffn_in': nrm(ks[18], (DEPTH, D_MODEL, 2 * D_FF), D_MODEL ** -0.5),
        'w_ffn_out': nrm(ks[19], (DEPTH, D_FF, D_MODEL), BETA * D_FF ** -0.5),
        'ln_mix_g': 1.0 + nrm(ks[20], (DEPTH, D_MODEL), 0.02),
        'ln_mix_b': nrm(ks[21], (DEPTH, D_MODEL), 0.02),
        'ln_ffn_g': 1.0 + nrm(ks[22], (DEPTH, D_MODEL), 0.02),
        'ln_ffn_b': nrm(ks[23], (DEPTH, D_MODEL), 0.02),
    }


def reference(x_prompt, x_sample, state_conv, cache_k, cache_v, cache_kidx, page_table, rel_bias,
              w_pw1, b_pw1, w_dw, b_dw, conv_norm_g, conv_norm_b, w_pw2, b_pw2,
              w_attn_in, w_attn_out, w_ffn_in, w_ffn_out, ln_mix_g, ln_mix_b, ln_ffn_g, ln_ffn_b):
    xp, xs = x_prompt, x_sample
    conv_p, conv_s = [], []
    k_p, v_p, ki_p, k_s, v_s, ki_s = [], [], [], [], [], []
    for i in range(DEPTH):
        j = i // 2
        if i % 2 == 0:
            cp = (w_pw1[j], b_pw1[j], w_dw[j], b_dw[j], conv_norm_g[j], conv_norm_b[j], w_pw2[j], b_pw2[j])
            zero_buf = jnp.zeros((xp.shape[0], CONV_STATE, D_MODEL), xp.dtype)
            mp, bp = conv_module(xp, zero_buf, *cp)
            ms, bs = conv_module(xs, state_conv[j], *cp)
            conv_p.append(bp)
            conv_s.append(bs)
        else:
            mp, kk, vv, kki = attn_prompt(xp, w_attn_in[j], w_attn_out[j], rel_bias)
            k_p.append(kk)
            v_p.append(vv)
            ki_p.append(kki)
            ms, kk, vv, kki = attn_sample(xs, cache_k, cache_v, cache_kidx, page_table, j,
                                          w_attn_in[j], w_attn_out[j], rel_bias)
            k_s.append(kk)
            v_s.append(vv)
            ki_s.append(kki)
        xp = layer_norm(ALPHA * xp + mp, ln_mix_g[i], ln_mix_b[i])
        xs = layer_norm(ALPHA * xs + ms, ln_mix_g[i], ln_mix_b[i])
        xp = layer_norm(ALPHA * xp + swiglu(xp, w_ffn_in[i], w_ffn_out[i]), ln_ffn_g[i], ln_ffn_b[i])
        xs = layer_norm(ALPHA * xs + swiglu(xs, w_ffn_in[i], w_ffn_out[i]), ln_ffn_g[i], ln_ffn_b[i])
    return (xp, xs, jnp.stack(conv_p), jnp.stack(conv_s), jnp.stack(k_p), jnp.stack(v_p), jnp.stack(ki_p),
            jnp.stack(k_s), jnp.stack(v_s), jnp.stack(ki_s))
```

```python
import functools
import math

import numpy as np
import jax
import jax.numpy as jnp
from jax import lax
from jax.experimental import pallas as pl
from jax.experimental.pallas import tpu as pltpu

F32, BF16, I32 = jnp.float32, jnp.bfloat16, jnp.int32

D_MODEL = 1024
N_HEADS = 8
HEAD_DIM = D_MODEL // N_HEADS
IDX_HEADS = 8
IDX_DIM = 64
TOPK_MAX = 256
CONV_WIDTH = 31
CONV_STATE = CONV_WIDTH - 1
D_FF = 2816
REL_BUCKETS = 32
REL_MAX_DIST = 128
PAGE_SIZE = 128
DEPTH = 2
ALPHA = (2 * DEPTH) ** 0.25
LN_EPS = 1e-5

LANES = 128
SUBLANES = 8
V7X_VMEM_LIMIT_BYTES = 56 * 1024 * 1024

LOG2E = math.log2(math.e)
NEG_INF = float("-inf")
M_FLOOR = -3.0e38
INT_MIN = -(2 ** 31)

BQ = 256
CH = 128
TK = 1024
BAND = CH + BQ

S1_PAGES = 8
S3_PAGES = 4


def _bucket_lower_bounds():
    max_exact = REL_BUCKETS // 2
    lows = [None] * REL_BUCKETS
    for d in range(0, REL_MAX_DIST + 1):
        if d < max_exact:
            b = d
        else:
            b = max_exact + int(math.log(d / max_exact) / math.log(REL_MAX_DIST / max_exact)
                                * (REL_BUCKETS - max_exact))
            b = min(b, REL_BUCKETS - 1)
        if lows[b] is None:
            lows[b] = d
    nxt = REL_MAX_DIST
    for b in range(REL_BUCKETS - 1, -1, -1):
        if lows[b] is None:
            lows[b] = nxt
        nxt = lows[b]
    return lows


BUCKET_LOW = _bucket_lower_bounds()


def _mm(a, b):
    return jnp.dot(a, b, preferred_element_type=F32)


def _mm_nt(a, b):
    return lax.dot_general(a, b, (((1,), (1,)), ((), ())), preferred_element_type=F32)


def _layer_norm(y, g, b):
    mu = jnp.mean(y, axis=-1, keepdims=True)
    d = y - mu
    var = jnp.mean(d * d, axis=-1, keepdims=True)
    return d * lax.rsqrt(var + LN_EPS) * g + b


def _params(n_axes):
    return pltpu.CompilerParams(dimension_semantics=("arbitrary",) * n_axes,
                                vmem_limit_bytes=V7X_VMEM_LIMIT_BYTES)


def _resident(shape):
    nd = len(shape)
    return pl.BlockSpec(shape, lambda *_: (0,) * nd, pipeline_mode=pl.Buffered(1))


def _row_tile(m):
    for t in (512, 256, 128, 64, 32, 16, 8):
        if m % t == 0:
            return t
    raise ValueError(f"row count {m} is not a multiple of 8")


def _bias_from_dist(dist, rb_ref, h):
    bias = jnp.full(dist.shape, rb_ref[0, h] * LOG2E, F32)
    for b in range(1, REL_BUCKETS):
        bias = jnp.where(dist >= BUCKET_LOW[b], rb_ref[b, h] * LOG2E, bias)
    return bias


def _key_to_f32(k):
    bits = jnp.where(k < 0, k ^ jnp.int32(0x7FFFFFFF), k)
    return lax.bitcast_convert_type(bits, F32)


def _f32_to_key(x):
    bits = lax.bitcast_convert_type(x, I32)
    return jnp.where(bits < 0, bits ^ jnp.int32(0x7FFFFFFF), bits)


def _glu_kernel(x_ref, w_ref, b_ref, u_ref):
    d = u_ref.shape[-1]
    xb = x_ref[...].astype(BF16)
    a = _mm(xb, w_ref[:, :d]) + b_ref[:, :d]
    g = _mm(xb, w_ref[:, d:]) + b_ref[:, d:]
    u_ref[...] = a * jax.nn.sigmoid(g)


def glu(x, w_bf, b):
    m, d = x.shape
    tm = _row_tile(m)
    return pl.pallas_call(
        _glu_kernel,
        grid=(m // tm,),
        in_specs=[pl.BlockSpec((tm, d), lambda i: (i, 0)), _resident((d, 2 * d)), _resident((1, 2 * d))],
        out_specs=pl.BlockSpec((tm, d), lambda i: (i, 0)),
        out_shape=jax.ShapeDtypeStruct((m, d), F32),
        compiler_params=_params(1),
        name="glu",
    )(x, w_bf, b.reshape(1, 2 * d))


CONV_HALO = 32
CONV_ROWS = 64


def _conv_prompt_kernel(u_ref, halo_ref, wdw_ref, bdw_ref, g_ref, b_ref, z_ref, win_ref, y_ref):
    i = pl.program_id(0)
    tm = u_ref.shape[0]
    ncb = win_ref.shape[0]
    halo = jnp.where(i > 0, halo_ref[...], 0.0)
    for cb in range(ncb):
        win_ref[cb, 0:CONV_HALO, :] = halo[:, cb * LANES:(cb + 1) * LANES]
        win_ref[cb, CONV_HALO:CONV_HALO + tm, :] = u_ref[:, cb * LANES:(cb + 1) * LANES]
    shift = CONV_HALO - CONV_STATE

    def cb_body(cb, carry):
        w = wdw_ref[cb]
        for r in range(tm // CONV_ROWS):
            acc = jnp.zeros((CONV_ROWS, LANES), F32)
            for j in range(CONV_WIDTH):
                acc = acc + win_ref[cb, pl.ds(r * CONV_ROWS + shift + j, CONV_ROWS), :] * w[j:j + 1, :]
            y_ref[cb, r * CONV_ROWS:(r + 1) * CONV_ROWS, :] = acc
        return carry

    lax.fori_loop(0, ncb, cb_body, 0)
    y = jnp.concatenate([y_ref[cb] for cb in range(ncb)], axis=1) + bdw_ref[...]
    z = _layer_norm(y, g_ref[...], b_ref[...])
    z_ref[...] = (z * jax.nn.sigmoid(z)).astype(BF16)


def conv_prompt(u, w_dw, b_dw, g_n, b_n):
    m, d = u.shape
    tm = _row_tile(m)
    ncb = d // LANES
    wdw = jnp.pad(w_dw, ((0, 1), (0, 0))).reshape(CONV_WIDTH + 1, ncb, LANES).transpose(1, 0, 2)
    hb = tm // CONV_HALO
    return pl.pallas_call(
        _conv_prompt_kernel,
        grid=(m // tm,),
        in_specs=[pl.BlockSpec((tm, d), lambda i: (i, 0)),
                  pl.BlockSpec((CONV_HALO, d), lambda i: (jnp.maximum(i * hb - 1, 0), 0)),
                  _resident((ncb, CONV_WIDTH + 1, LANES)),
                  _resident((1, d)), _resident((1, d)), _resident((1, d))],
        out_specs=pl.BlockSpec((tm, d), lambda i: (i, 0)),
        out_shape=jax.ShapeDtypeStruct((m, d), BF16),
        scratch_shapes=[pltpu.VMEM((ncb, tm + CONV_HALO, LANES), F32), pltpu.VMEM((ncb, tm, LANES), F32)],
        compiler_params=_params(1),
        name="conv_prompt",
    )(u, u, wdw, b_dw.reshape(1, d), g_n.reshape(1, d), b_n.reshape(1, d))


def _conv_sample_kernel(st_ref, u_ref, wdw_ref, bdw_ref, g_ref, b_ref, z_ref, ns_ref, win_ref):
    t = u_ref.shape[0]
    win_ref[0:CONV_STATE, :] = st_ref[...]
    win_ref[CONV_STATE:CONV_STATE + t, :] = u_ref[...]
    acc = jnp.zeros(u_ref.shape, F32)
    for j in range(CONV_WIDTH):
        acc = acc + win_ref[j:j + t, :] * wdw_ref[j:j + 1, :]
    z = _layer_norm(acc + bdw_ref[...], g_ref[...], b_ref[...])
    z_ref[...] = z * jax.nn.sigmoid(z)
    ns_ref[...] = win_ref[t:t + CONV_STATE, :]


def conv_sample(u, state, w_dw, b_dw, g_n, b_n):
    b, t, d = u.shape
    return pl.pallas_call(
        _conv_sample_kernel,
        grid=(b,),
        in_specs=[pl.BlockSpec((None, CONV_STATE, d), lambda i: (i, 0, 0)),
                  pl.BlockSpec((None, t, d), lambda i: (i, 0, 0)),
                  _resident((CONV_WIDTH + 1, d)), _resident((1, d)), _resident((1, d)), _resident((1, d))],
        out_specs=[pl.BlockSpec((None, t, d), lambda i: (i, 0, 0)),
                   pl.BlockSpec((None, CONV_STATE, d), lambda i: (i, 0, 0))],
        out_shape=[jax.ShapeDtypeStruct((b, t, d), F32), jax.ShapeDtypeStruct((b, CONV_STATE, d), F32)],
        scratch_shapes=[pltpu.VMEM((CONV_STATE + t + 2, d), F32)],
        compiler_params=_params(1),
        name="conv_sample",
    )(state, u, jnp.pad(w_dw, ((0, 1), (0, 0))), b_dw.reshape(1, d), g_n.reshape(1, d), b_n.reshape(1, d))


def _proj_res_ln_kernel(z_ref, w_ref, bias_ref, x_ref, g_ref, b_ref, o_ref):
    m = _mm(z_ref[...].astype(BF16), w_ref[...]) + bias_ref[...]
    o_ref[...] = _layer_norm(ALPHA * x_ref[...] + m, g_ref[...], b_ref[...])


def proj_res_ln(z, w_bf, bias, x, g, b):
    m, d = x.shape
    k = z.shape[1]
    tm = _row_tile(m)
    return pl.pallas_call(
        _proj_res_ln_kernel,
        grid=(m // tm,),
        in_specs=[pl.BlockSpec((tm, k), lambda i: (i, 0)), _resident((k, d)), _resident((1, d)),
                  pl.BlockSpec((tm, d), lambda i: (i, 0)), _resident((1, d)), _resident((1, d))],
        out_specs=pl.BlockSpec((tm, d), lambda i: (i, 0)),
        out_shape=jax.ShapeDtypeStruct((m, d), F32),
        compiler_params=_params(1),
        name="proj_res_ln",
    )(z, w_bf, bias.reshape(1, d), x, g.reshape(1, d), b.reshape(1, d))


FFN_CHUNK = 256


def _ffn_kernel(x_ref, win_ref, wout_ref, g_ref, b_ref, o_ref):
    x = x_ref[...]
    xb = x.astype(BF16)
    dff = wout_ref.shape[0]
    acc = jnp.zeros(x.shape, F32)
    for c in range(dff // FFN_CHUNK):
        lo = c * FFN_CHUNK
        gate = _mm(xb, win_ref[:, lo:lo + FFN_CHUNK])
        up = _mm(xb, win_ref[:, dff + lo:dff + lo + FFN_CHUNK])
        act = (gate * jax.nn.sigmoid(gate) * up).astype(BF16)
        acc = acc + _mm(act, wout_ref[lo:lo + FFN_CHUNK, :])
    o_ref[...] = _layer_norm(ALPHA * x + acc, g_ref[...], b_ref[...])


def ffn(x, win_bf, wout_bf, g, b):
    m, d = x.shape
    dff = wout_bf.shape[0]
    tm = _row_tile(m)
    return pl.pallas_call(
        _ffn_kernel,
        grid=(m // tm,),
        in_specs=[pl.BlockSpec((tm, d), lambda i: (i, 0)), _resident((d, 2 * dff)), _resident((dff, d)),
                  _resident((1, d)), _resident((1, d))],
        out_specs=pl.BlockSpec((tm, d), lambda i: (i, 0)),
        out_shape=jax.ShapeDtypeStruct((m, d), F32),
        compiler_params=_params(1),
        name="ffn",
    )(x, win_bf, wout_bf, g.reshape(1, d), b.reshape(1, d))


Q_SCALE = HEAD_DIM ** -0.5 * LOG2E
W_SCALE = IDX_HEADS ** -0.5 * IDX_DIM ** -0.5
QI_PAD = IDX_HEADS * LANES


def _split_attn_weights(w_in):
    d = D_MODEL
    o3 = 3 * d
    o4 = o3 + IDX_HEADS * IDX_DIM
    o5 = o4 + IDX_DIM
    wq, wk, wv = w_in[:, :d], w_in[:, d:2 * d], w_in[:, 2 * d:o3]
    wqi = w_in[:, o3:o4].reshape(d, IDX_HEADS, IDX_DIM)
    wqi = jnp.pad(wqi, ((0, 0), (0, 0), (0, LANES - IDX_DIM))).reshape(d, QI_PAD)
    wkw = jnp.pad(w_in[:, o4:], ((0, 0), (0, LANES - IDX_DIM - IDX_HEADS)))
    return tuple(a.astype(BF16) for a in (wq, wk, wv, wqi, wkw))


def _attn_proj_t_kernel(x_ref, wk_ref, wv_ref, wkw_ref, wqT_ref, wvT_ref, wqiT_ref, wwT_ref,
                        k32_ref, v32_ref, kiw_ref, qT_ref, kbf_ref, vT_ref, qiT_ref, kip_ref, wT_ref):
    xb = x_ref[...].astype(BF16)
    k = _mm(xb, wk_ref[...])
    k32_ref[...] = k
    kbf_ref[...] = k.astype(BF16)
    v32_ref[...] = _mm(xb, wv_ref[...])
    kiw = _mm(xb, wkw_ref[...])
    kiw_ref[...] = kiw
    lane = lax.broadcasted_iota(I32, kiw.shape, 1)
    kip_ref[...] = jnp.where(lane < IDX_DIM, kiw, 0.0).astype(BF16)
    qT_ref[...] = (_mm_nt(wqT_ref[...], xb) * Q_SCALE).astype(BF16)
    vT_ref[...] = _mm_nt(wvT_ref[...], xb).astype(BF16)
    qiT_ref[...] = _mm_nt(wqiT_ref[...], xb).astype(BF16)
    wT_ref[...] = _mm_nt(wwT_ref[...], xb) * W_SCALE


def attn_proj_t(x, w_in):
    m, d = x.shape
    tm = _row_tile(m)
    wq, wk, wv, wqi, wkw = _split_attn_weights(w_in)
    o5 = 3 * d + IDX_HEADS * IDX_DIM + IDX_DIM
    wwT = jnp.pad(w_in[:, o5:].T, ((0, 16 - IDX_HEADS), (0, 0))).astype(BF16)
    row = lambda n: pl.BlockSpec((tm, n), lambda i: (i, 0))
    col = lambda n: pl.BlockSpec((n, tm), lambda i: (0, i))
    return pl.pallas_call(
        _attn_proj_t_kernel,
        grid=(m // tm,),
        in_specs=[row(d), _resident((d, d)), _resident((d, d)), _resident((d, LANES)),
                  _resident((d, d)), _resident((d, d)), _resident((QI_PAD, d)), _resident((16, d))],
        out_specs=[row(d), row(d), row(LANES), col(d), row(d), col(d), col(QI_PAD), row(LANES), col(16)],
        out_shape=[jax.ShapeDtypeStruct((m, d), F32), jax.ShapeDtypeStruct((m, d), F32),
                   jax.ShapeDtypeStruct((m, LANES), F32), jax.ShapeDtypeStruct((d, m), BF16),
                   jax.ShapeDtypeStruct((m, d), BF16), jax.ShapeDtypeStruct((d, m), BF16),
                   jax.ShapeDtypeStruct((QI_PAD, m), BF16), jax.ShapeDtypeStruct((m, LANES), BF16),
                   jax.ShapeDtypeStruct((16, m), F32)],
        compiler_params=_params(1),
        name="attn_proj_t",
    )(x, wk, wv, wkw, wq.T, wv.T, wqi.T, wwT)


def _attn_proj_s_kernel(x_ref, wq_ref, wk_ref, wv_ref, wqi_ref, wkw_ref,
                        q_ref, k32_ref, v32_ref, qi_ref, kiw_ref):
    xb = x_ref[...].astype(BF16)
    q_ref[...] = (_mm(xb, wq_ref[...]) * Q_SCALE).astype(BF16)
    k32_ref[...] = _mm(xb, wk_ref[...])
    v32_ref[...] = _mm(xb, wv_ref[...])
    qi_ref[...] = _mm(xb, wqi_ref[...]).astype(BF16)
    kiw_ref[...] = _mm(xb, wkw_ref[...])


def attn_proj_s(x, w_in):
    m, d = x.shape
    tm = _row_tile(m)
    wq, wk, wv, wqi, wkw = _split_attn_weights(w_in)
    row = lambda n: pl.BlockSpec((tm, n), lambda i: (i, 0))
    return pl.pallas_call(
        _attn_proj_s_kernel,
        grid=(m // tm,),
        in_specs=[row(d), _resident((d, d)), _resident((d, d)), _resident((d, d)),
                  _resident((d, QI_PAD)), _resident((d, LANES))],
        out_specs=[row(d), row(d), row(d), row(QI_PAD), row(LANES)],
        out_shape=[jax.ShapeDtypeStruct((m, d), BF16), jax.ShapeDtypeStruct((m, d), F32),
                   jax.ShapeDtypeStruct((m, d), F32), jax.ShapeDtypeStruct((m, QI_PAD), BF16),
                   jax.ShapeDtypeStruct((m, LANES), F32)],
        compiler_params=_params(1),
        name="attn_proj_s",
    )(x, wq, wk, wv, wqi, wkw)


def _prompt_schedule(t):
    qb_of, kt_of, fl = [], [], []
    for qb in range(t // BQ):
        nb = -(-max(qb * BQ - CH, 0) // TK)
        qb_of.append(qb); kt_of.append(0); fl.append(1 | (2 if nb == 0 else 0))
        for kt in range(nb):
            qb_of.append(qb); kt_of.append(kt); fl.append(2 if kt == nb - 1 else 0)
    return (np.asarray(qb_of, np.int32), np.asarray(kt_of, np.int32), np.asarray(fl, np.int32))


def _attn_prompt_kernel(qb_ref, kt_ref, fl_ref,
                        qT_ref, qiT_ref, wT_ref, kip_ref, kb_ref, vTb_ref, kp_ref, kd_ref, vTp_ref, vTd_ref,
                        rb_ref, o_ref,
                        s_ref, thr_ref, bb_ref, m_ref, l_ref, acc_ref, madd_ref, *, k_sel, idx_bits):
    step = pl.program_id(0)
    qb = qb_ref[step]
    kt = kt_ref[step]
    flags = fl_ref[step]
    is_band = (flags & 1) != 0
    is_last = (flags & 2) != 0
    q0 = qb * BQ
    nch = 2 * (qb + 1)
    cband = 2 * qb - 1
    kf = float(k_sel)

    @pl.when(step == 0)
    def _build_band_bias():
        r = lax.broadcasted_iota(I32, (BAND, BQ), 0)
        j = lax.broadcasted_iota(I32, (BAND, BQ), 1)
        dist = CH + j - r
        for h in range(N_HEADS):
            bb_ref[h] = _bias_from_dist(dist, rb_ref, h)

    def attend(h, kh, vth, madd, bias, cbias):
        hs = slice(h * HEAD_DIM, (h + 1) * HEAD_DIM)
        lt = _mm(kh, qT_ref[hs, :])
        if bias is not None:
            lt = lt + bias
        lt = lt + madd
        mx = jnp.max(lt, axis=0, keepdims=True) + cbias
        m_old = m_ref[h:h + 1, :]
        m_new = jnp.maximum(m_old, mx)
        alpha = jnp.exp2(m_old - m_new)
        p = jnp.exp2(lt - (m_new - cbias))
        psum = p[0:SUBLANES]
        for i in range(1, p.shape[0] // SUBLANES):
            psum = psum + p[i * SUBLANES:(i + 1) * SUBLANES]
        l_ref[h] = alpha * l_ref[h] + psum
        acc_ref[hs, :] = alpha * acc_ref[hs, :] + _mm(vth, p.astype(BF16))
        m_ref[h:h + 1, :] = m_new

    @pl.when(is_band)
    def _band_step():
        for h in range(IDX_HEADS):
            wq = qiT_ref[h * LANES:(h + 1) * LANES, :]
            wrow = wT_ref[h:h + 1, :]

            def idx_body(c, carry, h=h, wq=wq, wrow=wrow):
                kic = kip_ref[pl.ds(pl.multiple_of(c * CH, CH), CH), :]
                val = jnp.maximum(_mm(kic, wq), 0.0) * wrow
                if h == 0:
                    s_ref[c] = val
                else:
                    s_ref[c] = s_ref[c] + val
                return carry

            lax.fori_loop(0, nch, idx_body, 0)
        rr = lax.broadcasted_iota(I32, (CH, BQ), 0)
        jj = lax.broadcasted_iota(I32, (CH, BQ), 1)
        for i in range(BQ // CH):
            c = 2 * qb + i
            s_ref[c] = jnp.where(i * CH + rr <= jj, s_ref[c], NEG_INF)

        def count(pred):
            def body(c, acc):
                ind = jnp.where(pred(c, s_ref[c]), 1.0, 0.0)
                part = ind[0:SUBLANES]
                for i in range(1, CH // SUBLANES):
                    part = part + ind[i * SUBLANES:(i + 1) * SUBLANES]
                return acc + part
            acc = lax.fori_loop(0, nch, body, jnp.zeros((SUBLANES, BQ), F32))
            return jnp.broadcast_to(jnp.sum(acc, axis=0, keepdims=True), (SUBLANES, BQ))

        def count_ge(cand_key):
            cand = _key_to_f32(cand_key)[0:1, :]
            return count(lambda c, x: x >= cand)

        total = (nch * CH).astype(F32)
        c0 = count(lambda c, x: x >= 0.0)
        ans = jnp.where(c0 >= kf, 0, INT_MIN).astype(I32)
        cnt = jnp.where(c0 >= kf, c0, total)

        def bit_body(i, carry):
            ans, cnt = carry
            cand = ans + jnp.left_shift(jnp.int32(1), 30 - i)
            c = count_ge(cand)
            ok = c >= kf
            return jnp.where(ok, cand, ans), jnp.where(ok, c, cnt)

        ans, cnt = lax.fori_loop(0, 31, bit_body, (ans, cnt))
        none = ans == INT_MIN
        thr = jnp.where(none, NEG_INF, _key_to_f32(ans))
        thr_ref[...] = thr

        has = jnp.logical_and(cnt > kf, jnp.logical_not(none))
        any_ties = jnp.max(jnp.where(has, 1.0, 0.0)) > 0.5

        @pl.when(any_ties)
        def _resolve_ties():
            t1 = thr[0:1, :]
            n_gt = count_ge(ans + 1)
            need = jnp.where(has, kf - n_gt, 3.0e38)
            rows = lax.broadcasted_iota(I32, (CH, BQ), 0)

            def j_body(i, j):
                cj = j + jnp.left_shift(jnp.int32(1), idx_bits - 1 - i)
                c1 = cj[0:1, :]
                f = count(lambda c, x: jnp.logical_and(x == t1, c * CH + rows < c1))
                return jnp.where(f < need, cj, j)

            jstar = lax.fori_loop(0, idx_bits, j_body, jnp.zeros((SUBLANES, BQ), I32))
            j1 = jstar[0:1, :]
            h1 = has[0:1, :]

            def demote(c, carry):
                x = s_ref[c]
                drop = jnp.logical_and(jnp.logical_and(x == t1, c * CH + rows > j1), h1)
                s_ref[c] = jnp.where(drop, NEG_INF, x)
                return carry

            lax.fori_loop(0, nch, demote, 0)

        m_ref[...] = jnp.full(m_ref.shape, M_FLOOR, F32)
        l_ref[...] = jnp.zeros(l_ref.shape, F32)
        acc_ref[...] = jnp.zeros(acc_ref.shape, F32)
        thr1 = thr_ref[0:1, :]
        cprev = jnp.maximum(cband, 0)
        sc = jnp.concatenate([s_ref[cprev], s_ref[2 * qb], s_ref[2 * qb + 1]], axis=0)
        r = lax.broadcasted_iota(I32, (BAND, BQ), 0)
        j = lax.broadcasted_iota(I32, (BAND, BQ), 1)
        ok = jnp.logical_and(sc >= thr1, r - CH <= j)
        ok = jnp.logical_and(ok, jnp.logical_or(r >= CH, qb > 0))
        madd = jnp.where(ok, 0.0, NEG_INF)
        for h in range(N_HEADS):
            hs = slice(h * HEAD_DIM, (h + 1) * HEAD_DIM)
            kh = jnp.concatenate([kp_ref[:, hs], kd_ref[:, hs]], axis=0)
            vth = jnp.concatenate([vTp_ref[hs, :], vTd_ref[hs, :]], axis=1)
            attend(h, kh, vth, madd, bb_ref[h], 0.0)

    @pl.when(jnp.logical_not(is_band))
    def _bulk_step():
        thr1 = thr_ref[0:1, :]
        for i in range(TK // CH):
            c = kt * (TK // CH) + i
            x = s_ref[jnp.minimum(c, cband - 1)]
            sel = jnp.where(x >= thr1, 0.0, NEG_INF)
            madd_ref[i * CH:(i + 1) * CH, :] = jnp.where(c < cband, sel, NEG_INF)
        for h in range(N_HEADS):
            hs = slice(h * HEAD_DIM, (h + 1) * HEAD_DIM)
            attend(h, kb_ref[:, hs], vTb_ref[hs, :], madd_ref[...], None,
                   rb_ref[REL_BUCKETS - 1, h] * LOG2E)

    @pl.when(is_last)
    def _finalize():
        for h in range(N_HEADS):
            hs = slice(h * HEAD_DIM, (h + 1) * HEAD_DIM)
            lsum = jnp.sum(l_ref[h], axis=0, keepdims=True)
            o_ref[:, hs] = (acc_ref[hs, :] / lsum).T.astype(BF16)


def attn_prompt(qT, qiT, wT, kip, kbf, vT, rel_bias):
    d, t = qT.shape
    k_sel = min(TOPK_MAX, t // 4)
    qb_of, kt_of, fl = _prompt_schedule(t)
    n_steps = len(qb_of)
    tk = min(TK, t)
    qcol = lambda n: pl.BlockSpec((n, BQ), lambda s, qb, kt, fl: (0, qb[s]))
    in_specs = [
        qcol(d), qcol(QI_PAD), qcol(16),
        pl.BlockSpec((t, LANES), lambda s, qb, kt, fl: (0, 0), pipeline_mode=pl.Buffered(1)),
        pl.BlockSpec((tk, d), lambda s, qb, kt, fl: (kt[s], 0)),
        pl.BlockSpec((d, tk), lambda s, qb, kt, fl: (0, kt[s])),
        pl.BlockSpec((CH, d), lambda s, qb, kt, fl: (jnp.maximum(2 * qb[s] - 1, 0), 0)),
        pl.BlockSpec((BQ, d), lambda s, qb, kt, fl: (qb[s], 0)),
        pl.BlockSpec((d, CH), lambda s, qb, kt, fl: (0, jnp.maximum(2 * qb[s] - 1, 0))),
        pl.BlockSpec((d, BQ), lambda s, qb, kt, fl: (0, qb[s])),
        pl.BlockSpec(memory_space=pltpu.SMEM),
    ]
    grid_spec = pltpu.PrefetchScalarGridSpec(
        num_scalar_prefetch=3,
        grid=(n_steps,),
        in_specs=in_specs,
        out_specs=pl.BlockSpec((BQ, d), lambda s, qb, kt, fl: (qb[s], 0)),
        scratch_shapes=[
            pltpu.VMEM((t // CH, CH, BQ), F32),
            pltpu.VMEM((SUBLANES, BQ), F32),
            pltpu.VMEM((N_HEADS, BAND, BQ), F32),
            pltpu.VMEM((N_HEADS, BQ), F32),
            pltpu.VMEM((N_HEADS, SUBLANES, BQ), F32),
            pltpu.VMEM((d, BQ), F32),
            pltpu.VMEM((tk, BQ), F32),
        ],
    )
    kern = functools.partial(_attn_prompt_kernel, k_sel=k_sel, idx_bits=max(t.bit_length() - 1, 1))
    return pl.pallas_call(
        kern,
        grid_spec=grid_spec,
        out_shape=jax.ShapeDtypeStruct((t, d), BF16),
        compiler_params=_params(1),
        name="attn_prompt",
    )(jnp.asarray(qb_of), jnp.asarray(kt_of), jnp.asarray(fl),
      qT, qiT, wT, kip, kbf, vT, kbf, kbf, vT, vT, rel_bias)


def _sample_scores_kernel(pt_ref, qi_ref, w_ref, *rest, n_groups):
    pages = rest[:S1_PAGES]
    kin_ref, o_ref, kbuf_ref = rest[S1_PAGES:]
    b = pl.program_id(0)
    g = pl.program_id(1)
    t = o_ref.shape[1]

    @pl.when(jnp.logical_and(b == 0, g == 0))
    def _zero_pad_lanes():
        kbuf_ref[...] = jnp.zeros(kbuf_ref.shape, BF16)

    def head_sum(val):
        s = val[0:t]
        for h in range(1, IDX_HEADS):
            s = s + val[h * t:(h + 1) * t]
        return s

    wcol = w_ref[:, 0:1]

    @pl.when(g < n_groups)
    def _past_pages():
        for i in range(S1_PAGES):
            kbuf_ref[i * PAGE_SIZE:(i + 1) * PAGE_SIZE, 0:IDX_DIM] = pages[i][...].astype(BF16)
        s = head_sum(jnp.maximum(_mm_nt(qi_ref[...], kbuf_ref[...]), 0.0) * wcol)
        for i in range(S1_PAGES):
            o_ref[i] = s[:, i * PAGE_SIZE:(i + 1) * PAGE_SIZE]

    @pl.when(g == n_groups)
    def _new_keys():
        s = head_sum(jnp.maximum(_mm_nt(qi_ref[...], kin_ref[...]), 0.0) * wcol)
        row = lax.broadcasted_iota(I32, s.shape, 0)
        lane = lax.broadcasted_iota(I32, s.shape, 1)
        o_ref[0] = jnp.where(lane <= row, s, NEG_INF)
        for i in range(1, S1_PAGES):
            o_ref[i] = jnp.full(s.shape, NEG_INF, F32)


def sample_scores(page_table, qi_ht, w_ht, cache_kidx, layer, ki_new):
    b, n_pages = page_table.shape
    t = qi_ht.shape[1] // IDX_HEADS
    n_groups = n_pages // S1_PAGES
    page_spec = lambda i: pl.BlockSpec(
        (None, None, PAGE_SIZE, IDX_DIM),
        lambda bi, g, pt: (layer, pt[bi, jnp.minimum(g, n_groups - 1) * S1_PAGES + i], 0, 0))
    grid_spec = pltpu.PrefetchScalarGridSpec(
        num_scalar_prefetch=1,
        grid=(b, n_groups + 1),
        in_specs=[pl.BlockSpec((None, IDX_HEADS * t, LANES), lambda bi, g, pt: (bi, 0, 0)),
                  pl.BlockSpec((None, IDX_HEADS * t, LANES), lambda bi, g, pt: (bi, 0, 0))]
                 + [page_spec(i) for i in range(S1_PAGES)]
                 + [pl.BlockSpec((None, PAGE_SIZE, LANES), lambda bi, g, pt: (bi, 0, 0))],
        out_specs=pl.BlockSpec((None, S1_PAGES, t, LANES), lambda bi, g, pt: (bi, g, 0, 0)),
        scratch_shapes=[pltpu.VMEM((S1_PAGES * PAGE_SIZE, LANES), BF16)],
    )
    return pl.pallas_call(
        functools.partial(_sample_scores_kernel, n_groups=n_groups),
        grid_spec=grid_spec,
        out_shape=jax.ShapeDtypeStruct((b, (n_groups + 1) * S1_PAGES, t, LANES), F32),
        compiler_params=_params(2),
        name="sample_scores",
    )(page_table, qi_ht, w_ht, *([cache_kidx] * S1_PAGES), ki_new)


def _sample_select_kernel(s_ref, o_ref, *, k_sel, past, idx_bits):
    x = s_ref[...]
    nck, t, _ = x.shape
    kf = float(k_sel)
    cidx = lax.broadcasted_iota(I32, x.shape, 0)
    row = lax.broadcasted_iota(I32, x.shape, 1)
    lane = lax.broadcasted_iota(I32, x.shape, 2)
    idx = cidx * LANES + lane

    def count(pred):
        per_lane = jnp.sum(jnp.where(pred, 1.0, 0.0), axis=0)
        return jnp.broadcast_to(jnp.sum(per_lane, axis=1, keepdims=True), (t, LANES))

    def count_ge(cand_key):
        return count(x >= _key_to_f32(cand_key)[None])

    total = float(nck * LANES)
    c0 = count_ge(jnp.zeros((t, LANES), I32))
    ans = jnp.where(c0 >= kf, 0, INT_MIN).astype(I32)
    cnt = jnp.where(c0 >= kf, c0, total)

    def bit_body(i, carry):
        ans, cnt = carry
        cand = ans + jnp.left_shift(jnp.int32(1), 30 - i)
        c = count_ge(cand)
        ok = c >= kf
        return jnp.where(ok, cand, ans), jnp.where(ok, c, cnt)

    ans, cnt = lax.fori_loop(0, 31, bit_body, (ans, cnt))
    none = ans == INT_MIN
    thr = jnp.where(none, NEG_INF, _key_to_f32(ans))
    has = jnp.logical_and(cnt > kf, jnp.logical_not(none))

    n_gt = count_ge(ans + 1)
    need = jnp.where(has, kf - n_gt, 3.0e38)
    eq = x == thr[None]

    def j_body(i, j):
        cj = j + jnp.left_shift(jnp.int32(1), idx_bits - 1 - i)
        f = count(jnp.logical_and(eq, idx < cj[None]))
        return jnp.where(f < need, cj, j)

    jstar = lax.fori_loop(0, idx_bits, j_body, jnp.zeros((t, LANES), I32))
    drop = jnp.logical_and(jnp.logical_and(eq, idx > jstar[None]), has[None])
    new = idx - past
    valid = jnp.logical_or(idx < past, jnp.logical_and(new < t, new <= row))
    keep = jnp.logical_and(jnp.logical_and(x >= thr[None], jnp.logical_not(drop)), valid)
    o_ref[...] = jnp.where(keep, 0.0, NEG_INF)


def sample_select(scores, past, k_sel):
    b, nck, t, _ = scores.shape
    spec = pl.BlockSpec((None, nck, t, LANES), lambda i: (i, 0, 0, 0))
    kern = functools.partial(_sample_select_kernel, k_sel=k_sel, past=past,
                             idx_bits=(nck * LANES - 1).bit_length())
    return pl.pallas_call(
        kern, grid=(b,), in_specs=[spec], out_specs=spec,
        out_shape=jax.ShapeDtypeStruct(scores.shape, F32),
        compiler_params=_params(1),
        name="sample_select",
    )(scores)


def _sample_attn_kernel(pt_ref, q_ref, madd_ref, *rest, n_groups, t):
    kpages = rest[:S3_PAGES]
    vpages = rest[S3_PAGES:2 * S3_PAGES]
    (kn_ref, vn_ref, rb_ref, o_ref,
     kcat_ref, vcat_ref, m_ref, l_ref, acc_ref, bconst_ref, blast_ref, bnew_ref) = rest[2 * S3_PAGES:]
    b = pl.program_id(0)
    g = pl.program_id(1)
    rows = N_HEADS * t

    @pl.when(jnp.logical_and(b == 0, g == 0))
    def _build_bias_tables():
        tq = lax.broadcasted_iota(I32, (t, LANES), 0)
        ln = lax.broadcasted_iota(I32, (t, LANES), 1)
        for h in range(N_HEADS):
            hs = slice(h * t, (h + 1) * t)
            bconst_ref[hs, :] = jnp.full((t, LANES), rb_ref[REL_BUCKETS - 1, h] * LOG2E, F32)
            blast_ref[hs, :] = _bias_from_dist(PAGE_SIZE + tq - ln, rb_ref, h)
            bnew_ref[hs, :] = _bias_from_dist(tq - ln, rb_ref, h)

    @pl.when(g == 0)
    def _init():
        m_ref[...] = jnp.full(m_ref.shape, M_FLOOR, F32)
        l_ref[...] = jnp.zeros(l_ref.shape, F32)
        acc_ref[...] = jnp.zeros(acc_ref.shape, F32)

    def tile_heads(m8):
        return jnp.concatenate([m8] * N_HEADS, axis=0)

    def attend(kk, vv, bias, madd):
        lg = _mm_nt(q_ref[...], kk) + bias + madd
        mx = jnp.max(lg, axis=1, keepdims=True)
        m_old = m_ref[...]
        m_new = jnp.maximum(m_old, mx)
        alpha = jnp.exp2(m_old - m_new)
        p = jnp.exp2(lg - m_new[:, 0:1])
        l_ref[...] = alpha * l_ref[...] + jnp.sum(p, axis=1, keepdims=True)
        acc_ref[...] = alpha[:, 0:1] * acc_ref[...] + _mm(p.astype(BF16), vv)
        m_ref[...] = m_new

    @pl.when(g < n_groups)
    def _cached_pages():
        for i in range(S3_PAGES):
            kcat_ref[i * PAGE_SIZE:(i + 1) * PAGE_SIZE, :] = kpages[i][...].astype(BF16)
            vcat_ref[i * PAGE_SIZE:(i + 1) * PAGE_SIZE, :] = vpages[i][...].astype(BF16)
        last = jnp.where(g == n_groups - 1, blast_ref[...], bconst_ref[...])
        bias = jnp.concatenate([bconst_ref[...]] * (S3_PAGES - 1) + [last], axis=1)
        madd = jnp.concatenate([tile_heads(madd_ref[i]) for i in range(S3_PAGES)], axis=1)
        attend(kcat_ref[...], vcat_ref[...], bias, madd)

    @pl.when(g == n_groups)
    def _new_keys_and_finish():
        attend(kn_ref[...], vn_ref[...], bnew_ref[...], tile_heads(madd_ref[0]))
        for h in range(N_HEADS):
            hs = slice(h * HEAD_DIM, (h + 1) * HEAD_DIM)
            rs = slice(h * t, (h + 1) * t)
            o_ref[:, hs] = acc_ref[rs, hs] / l_ref[rs, 0:1]


def sample_attn(page_table, q_bd, madd, cache_k, cache_v, layer, k_new, v_new, rel_bias):
    b, n_pages = page_table.shape
    rows, d = q_bd.shape[1:]
    t = rows // N_HEADS
    n_groups = n_pages // S3_PAGES
    page_spec = lambda i: pl.BlockSpec(
        (None, None, PAGE_SIZE, d),
        lambda bi, g, pt: (layer, pt[bi, jnp.minimum(g, n_groups - 1) * S3_PAGES + i], 0, 0))
    per_b = lambda shape: pl.BlockSpec((None,) + shape, lambda bi, g, pt: (bi,) + (0,) * len(shape))
    grid_spec = pltpu.PrefetchScalarGridSpec(
        num_scalar_prefetch=1,
        grid=(b, n_groups + 1),
        in_specs=[per_b((rows, d)),
                  pl.BlockSpec((None, S3_PAGES, t, LANES), lambda bi, g, pt: (bi, g, 0, 0))]
                 + [page_spec(i) for i in range(S3_PAGES)] * 2
                 + [per_b((PAGE_SIZE, d)), per_b((PAGE_SIZE, d)), pl.BlockSpec(memory_space=pltpu.SMEM)],
        out_specs=per_b((t, d)),
        scratch_shapes=[pltpu.VMEM((S3_PAGES * PAGE_SIZE, d), BF16), pltpu.VMEM((S3_PAGES * PAGE_SIZE, d), BF16),
                        pltpu.VMEM((rows, LANES), F32), pltpu.VMEM((rows, LANES), F32),
                        pltpu.VMEM((rows, d), F32),
                        pltpu.VMEM((rows, LANES), F32), pltpu.VMEM((rows, LANES), F32),
                        pltpu.VMEM((rows, LANES), F32)],
    )
    return pl.pallas_call(
        functools.partial(_sample_attn_kernel, n_groups=n_groups, t=t),
        grid_spec=grid_spec,
        out_shape=jax.ShapeDtypeStruct((b, t, d), F32),
        compiler_params=_params(2),
        name="sample_attn",
    )(page_table, q_bd, madd, *([cache_k] * S3_PAGES), *([cache_v] * S3_PAGES), k_new, v_new, rel_bias)


def attn_sample(x, cache_k, cache_v, cache_kidx, page_table, layer, w_in, rel_bias):
    b, t, d = x.shape
    n_pool = cache_k.shape[1]
    past = page_table.shape[1] * PAGE_SIZE
    k_sel = min(TOPK_MAX, (past + t) // 4)
    q, k32, v32, qi, kiw = attn_proj_s(x.reshape(b * t, d), w_in)
    qi_ht = qi.reshape(b, t, IDX_HEADS, LANES).transpose(0, 2, 1, 3).reshape(b, IDX_HEADS * t, LANES)
    wi = kiw[:, IDX_DIM:IDX_DIM + IDX_HEADS] * W_SCALE
    w_ht = jnp.broadcast_to(wi.reshape(b, t, IDX_HEADS).transpose(0, 2, 1).reshape(b, IDX_HEADS * t, 1),
                            (b, IDX_HEADS * t, LANES))
    head_of_col = jnp.arange(d, dtype=I32) // HEAD_DIM
    q_bd = jnp.where(head_of_col[None, None, None, :] == jnp.arange(N_HEADS, dtype=I32)[None, :, None, None],
                     q.reshape(b, 1, t, d), jnp.zeros((), BF16)).reshape(b, N_HEADS * t, d)
    pad_rows = lambda a: jnp.pad(a.reshape(b, t, -1), ((0, 0), (0, PAGE_SIZE - t), (0, 0))).astype(BF16)
    ki_new = pad_rows(jnp.pad(kiw[:, :IDX_DIM], ((0, 0), (0, LANES - IDX_DIM))))
    scores = sample_scores(page_table, qi_ht, w_ht, cache_kidx, layer, ki_new)
    madd = sample_select(scores, past, k_sel)
    out = sample_attn(page_table, q_bd, madd,
                      cache_k.reshape(cache_k.shape[0], n_pool, PAGE_SIZE, d),
                      cache_v.reshape(cache_v.shape[0], n_pool, PAGE_SIZE, d),
                      layer, pad_rows(k32), pad_rows(v32), rel_bias)
    return out.reshape(b * t, d), k32, v32, kiw[:, :IDX_DIM]


def kernel(x_prompt, x_sample, state_conv, cache_k, cache_v, cache_kidx, page_table, rel_bias,
           w_pw1, b_pw1, w_dw, b_dw, conv_norm_g, conv_norm_b, w_pw2, b_pw2,
           w_attn_in, w_attn_out, w_ffn_in, w_ffn_out, ln_mix_g, ln_mix_b, ln_ffn_g, ln_ffn_b):
    bp, tp, d = x_prompt.shape
    bs, ts, _ = x_sample.shape
    assert bp == 1 and d == D_MODEL and tp % TK == 0
    xp = x_prompt.reshape(tp, d)
    xs = x_sample.reshape(bs * ts, d)
    bf = lambda a: a.astype(BF16)

    w1, w2 = bf(w_pw1[0]), bf(w_pw2[0])
    up = glu(xp, w1, b_pw1[0])
    us = glu(xs, w1, b_pw1[0])
    zp = conv_prompt(up, w_dw[0], b_dw[0], conv_norm_g[0], conv_norm_b[0])
    zs, conv_s = conv_sample(us.reshape(bs, ts, d), state_conv[0], w_dw[0], b_dw[0],
                             conv_norm_g[0], conv_norm_b[0])
    conv_p = up[tp - CONV_STATE:]
    xp = proj_res_ln(zp, w2, b_pw2[0], xp, ln_mix_g[0], ln_mix_b[0])
    xs = proj_res_ln(zs.reshape(bs * ts, d), w2, b_pw2[0], xs, ln_mix_g[0], ln_mix_b[0])
    wf_in, wf_out = bf(w_ffn_in[0]), bf(w_ffn_out[0])
    xp = ffn(xp, wf_in, wf_out, ln_ffn_g[0], ln_ffn_b[0])
    xs = ffn(xs, wf_in, wf_out, ln_ffn_g[0], ln_ffn_b[0])

    w_o = bf(w_attn_out[0])
    no_bias = jnp.zeros((d,), F32)
    kp32, vp32, kiwp, qT, kbf, vT, qiT, kip, wT = attn_proj_t(xp, w_attn_in[0])
    ap = attn_prompt(qT, qiT, wT, kip, kbf, vT, rel_bias)
    xp = proj_res_ln(ap, w_o, no_bias, xp, ln_mix_g[1], ln_mix_b[1])
    a_s, ks32, vs32, kis = attn_sample(xs.reshape(bs, ts, d), cache_k, cache_v, cache_kidx, page_table, 0,
                                       w_attn_in[0], rel_bias)
    xs = proj_res_ln(a_s, w_o, no_bias, xs, ln_mix_g[1], ln_mix_b[1])
    wf_in, wf_out = bf(w_ffn_in[1]), bf(w_ffn_out[1])
    xp = ffn(xp, wf_in, wf_out, ln_ffn_g[1], ln_ffn_b[1])
    xs = ffn(xs, wf_in, wf_out, ln_ffn_g[1], ln_ffn_b[1])

    return (xp.reshape(1, tp, d), xs.reshape(bs, ts, d),
            conv_p.reshape(1, 1, CONV_STATE, d), conv_s.reshape(1, bs, CONV_STATE, d),
            kp32.reshape(1, 1, tp, N_HEADS, HEAD_DIM), vp32.reshape(1, 1, tp, N_HEADS, HEAD_DIM),
            kiwp[:, :IDX_DIM].reshape(1, 1, tp, IDX_DIM),
            ks32.reshape(1, bs, ts, N_HEADS, HEAD_DIM), vs32.reshape(1, bs, ts, N_HEADS, HEAD_DIM),
            kis.reshape(1, bs, ts, IDX_DIM))
```

```python
import functools
import math

import numpy as np
import jax
import jax.numpy as jnp
from jax import lax
from jax.experimental import pallas as pl
from jax.experimental.pallas import tpu as pltpu

F32, BF16, I32 = jnp.float32, jnp.bfloat16, jnp.int32

D_MODEL = 1024
N_HEADS = 8
HEAD_DIM = D_MODEL // N_HEADS
IDX_HEADS = 8
IDX_DIM = 64
TOPK_MAX = 256
CONV_WIDTH = 31
CONV_STATE = CONV_WIDTH - 1
D_FF = 2816
REL_BUCKETS = 32
REL_MAX_DIST = 128
PAGE_SIZE = 128
DEPTH = 2
ALPHA = (2 * DEPTH) ** 0.25
LN_EPS = 1e-5

LANES = 128
SUBLANES = 8
V7X_VMEM_LIMIT_BYTES = 56 * 1024 * 1024

LOG2E = math.log2(math.e)
NEG_INF = float("-inf")
M_FLOOR = -3.0e38
INT_MIN = -(2 ** 31)

BQ = 256
CH = 128
TK = 1024
IDX_GROUP = 4
BAND = CH + BQ

S1_PAGES = 16
S3_PAGES = 8


def _bucket_lower_bounds():
    max_exact = REL_BUCKETS // 2
    lows = [None] * REL_BUCKETS
    for d in range(0, REL_MAX_DIST + 1):
        if d < max_exact:
            b = d
        else:
            b = max_exact + int(math.log(d / max_exact) / math.log(REL_MAX_DIST / max_exact)
                                * (REL_BUCKETS - max_exact))
            b = min(b, REL_BUCKETS - 1)
        if lows[b] is None:
            lows[b] = d
    nxt = REL_MAX_DIST
    for b in range(REL_BUCKETS - 1, -1, -1):
        if lows[b] is None:
            lows[b] = nxt
        nxt = lows[b]
    return lows


BUCKET_LOW = _bucket_lower_bounds()


def _mm(a, b):
    return jnp.dot(a, b, preferred_element_type=F32)


def _mm_nt(a, b):
    return lax.dot_general(a, b, (((1,), (1,)), ((), ())), preferred_element_type=F32)


def _layer_norm(y, g, b):
    mu = jnp.mean(y, axis=-1, keepdims=True)
    d = y - mu
    var = jnp.mean(d * d, axis=-1, keepdims=True)
    return d * lax.rsqrt(var + LN_EPS) * g + b


def _params(n_axes):
    return pltpu.CompilerParams(dimension_semantics=("arbitrary",) * n_axes,
                                vmem_limit_bytes=V7X_VMEM_LIMIT_BYTES)


def _resident(shape):
    nd = len(shape)
    return pl.BlockSpec(shape, lambda *_: (0,) * nd, pipeline_mode=pl.Buffered(1))


def _row_tile(m):
    for t in (512, 256, 128, 64, 32, 16, 8):
        if m % t == 0:
            return t
    raise ValueError(f"row count {m} is not a multiple of 8")


def _bias_from_dist(dist, rb_ref, h):
    bias = jnp.full(dist.shape, rb_ref[0, h] * LOG2E, F32)
    for b in range(1, REL_BUCKETS):
        bias = jnp.where(dist >= BUCKET_LOW[b], rb_ref[b, h] * LOG2E, bias)
    return bias


def _key_to_f32(k):
    bits = jnp.where(k < 0, k ^ jnp.int32(0x7FFFFFFF), k)
    return lax.bitcast_convert_type(bits, F32)


def _glu_kernel(x_ref, w_ref, b_ref, u_ref):
    d = u_ref.shape[-1]
    xb = x_ref[...].astype(BF16)
    a = _mm(xb, w_ref[:, :d]) + b_ref[:, :d]
    g = _mm(xb, w_ref[:, d:]) + b_ref[:, d:]
    u_ref[...] = a * jax.nn.sigmoid(g)


def glu(x, w_bf, b):
    m, d = x.shape
    tm = _row_tile(m)
    return pl.pallas_call(
        _glu_kernel,
        grid=(m // tm,),
        in_specs=[pl.BlockSpec((tm, d), lambda i: (i, 0)), _resident((d, 2 * d)), _resident((1, 2 * d))],
        out_specs=pl.BlockSpec((tm, d), lambda i: (i, 0)),
        out_shape=jax.ShapeDtypeStruct((m, d), F32),
        compiler_params=_params(1),
        name="glu",
    )(x, w_bf, b.reshape(1, 2 * d))


CONV_HALO = 32
CONV_ROWS = 64


def _conv_prompt_kernel(u_ref, halo_ref, wdw_ref, bdw_ref, g_ref, b_ref, z_ref, win_ref, y_ref):
    i = pl.program_id(0)
    tm = u_ref.shape[0]
    ncb = win_ref.shape[0]
    halo = jnp.where(i > 0, halo_ref[...], 0.0)
    for cb in range(ncb):
        win_ref[cb, 0:CONV_HALO, :] = halo[:, cb * LANES:(cb + 1) * LANES]
        win_ref[cb, CONV_HALO:CONV_HALO + tm, :] = u_ref[:, cb * LANES:(cb + 1) * LANES]
    shift = CONV_HALO - CONV_STATE

    def cb_body(cb, carry):
        w = wdw_ref[cb]
        for r in range(tm // CONV_ROWS):
            acc = jnp.zeros((CONV_ROWS, LANES), F32)
            for j in range(CONV_WIDTH):
                acc = acc + win_ref[cb, pl.ds(r * CONV_ROWS + shift + j, CONV_ROWS), :] * w[j:j + 1, :]
            y_ref[cb, r * CONV_ROWS:(r + 1) * CONV_ROWS, :] = acc
        return carry

    lax.fori_loop(0, ncb, cb_body, 0)
    y = jnp.concatenate([y_ref[cb] for cb in range(ncb)], axis=1) + bdw_ref[...]
    z = _layer_norm(y, g_ref[...], b_ref[...])
    z_ref[...] = (z * jax.nn.sigmoid(z)).astype(BF16)


def conv_prompt(u, w_dw, b_dw, g_n, b_n):
    m, d = u.shape
    tm = _row_tile(m)
    ncb = d // LANES
    wdw = jnp.pad(w_dw, ((0, 1), (0, 0))).reshape(CONV_WIDTH + 1, ncb, LANES).transpose(1, 0, 2)
    hb = tm // CONV_HALO
    return pl.pallas_call(
        _conv_prompt_kernel,
        grid=(m // tm,),
        in_specs=[pl.BlockSpec((tm, d), lambda i: (i, 0)),
                  pl.BlockSpec((CONV_HALO, d), lambda i: (jnp.maximum(i * hb - 1, 0), 0)),
                  _resident((ncb, CONV_WIDTH + 1, LANES)),
                  _resident((1, d)), _resident((1, d)), _resident((1, d))],
        out_specs=pl.BlockSpec((tm, d), lambda i: (i, 0)),
        out_shape=jax.ShapeDtypeStruct((m, d), BF16),
        scratch_shapes=[pltpu.VMEM((ncb, tm + CONV_HALO, LANES), F32), pltpu.VMEM((ncb, tm, LANES), F32)],
        compiler_params=_params(1),
        name="conv_prompt",
    )(u, u, wdw, b_dw.reshape(1, d), g_n.reshape(1, d), b_n.reshape(1, d))


def _conv_sample_kernel(st_ref, u_ref, wdw_ref, bdw_ref, g_ref, b_ref, z_ref, ns_ref, win_ref):
    t = u_ref.shape[0]
    win_ref[0:CONV_STATE, :] = st_ref[...]
    win_ref[CONV_STATE:CONV_STATE + t, :] = u_ref[...]
    acc = jnp.zeros(u_ref.shape, F32)
    for j in range(CONV_WIDTH):
        acc = acc + win_ref[j:j + t, :] * wdw_ref[j:j + 1, :]
    z = _layer_norm(acc + bdw_ref[...], g_ref[...], b_ref[...])
    z_ref[...] = z * jax.nn.sigmoid(z)
    ns_ref[...] = win_ref[t:t + CONV_STATE, :]


def conv_sample(u, state, w_dw, b_dw, g_n, b_n):
    b, t, d = u.shape
    return pl.pallas_call(
        _conv_sample_kernel,
        grid=(b,),
        in_specs=[pl.BlockSpec((None, CONV_STATE, d), lambda i: (i, 0, 0)),
                  pl.BlockSpec((None, t, d), lambda i: (i, 0, 0)),
                  _resident((CONV_WIDTH + 1, d)), _resident((1, d)), _resident((1, d)), _resident((1, d))],
        out_specs=[pl.BlockSpec((None, t, d), lambda i: (i, 0, 0)),
                   pl.BlockSpec((None, CONV_STATE, d), lambda i: (i, 0, 0))],
        out_shape=[jax.ShapeDtypeStruct((b, t, d), F32), jax.ShapeDtypeStruct((b, CONV_STATE, d), F32)],
        scratch_shapes=[pltpu.VMEM((CONV_STATE + t + 2, d), F32)],
        compiler_params=_params(1),
        name="conv_sample",
    )(state, u, jnp.pad(w_dw, ((0, 1), (0, 0))), b_dw.reshape(1, d), g_n.reshape(1, d), b_n.reshape(1, d))


def _proj_res_ln_kernel(z_ref, w_ref, bias_ref, x_ref, g_ref, b_ref, o_ref):
    m = _mm(z_ref[...].astype(BF16), w_ref[...]) + bias_ref[...]
    o_ref[...] = _layer_norm(ALPHA * x_ref[...] + m, g_ref[...], b_ref[...])


def proj_res_ln(z, w_bf, bias, x, g, b):
    m, d = x.shape
    k = z.shape[1]
    tm = _row_tile(m)
    return pl.pallas_call(
        _proj_res_ln_kernel,
        grid=(m // tm,),
        in_specs=[pl.BlockSpec((tm, k), lambda i: (i, 0)), _resident((k, d)), _resident((1, d)),
                  pl.BlockSpec((tm, d), lambda i: (i, 0)), _resident((1, d)), _resident((1, d))],
        out_specs=pl.BlockSpec((tm, d), lambda i: (i, 0)),
        out_shape=jax.ShapeDtypeStruct((m, d), F32),
        compiler_params=_params(1),
        name="proj_res_ln",
    )(z, w_bf, bias.reshape(1, d), x, g.reshape(1, d), b.reshape(1, d))


FFN_CHUNK = 256


def _ffn_kernel(x_ref, win_ref, wout_ref, g_ref, b_ref, o_ref):
    x = x_ref[...]
    xb = x.astype(BF16)
    dff = wout_ref.shape[0]
    acc = jnp.zeros(x.shape, F32)
    for c in range(dff // FFN_CHUNK):
        lo = c * FFN_CHUNK
        gate = _mm(xb, win_ref[:, lo:lo + FFN_CHUNK])
        up = _mm(xb, win_ref[:, dff + lo:dff + lo + FFN_CHUNK])
        act = (gate * jax.nn.sigmoid(gate) * up).astype(BF16)
        acc = acc + _mm(act, wout_ref[lo:lo + FFN_CHUNK, :])
    o_ref[...] = _layer_norm(ALPHA * x + acc, g_ref[...], b_ref[...])


def ffn(x, win_bf, wout_bf, g, b):
    m, d = x.shape
    dff = wout_bf.shape[0]
    tm = _row_tile(m)
    return pl.pallas_call(
        _ffn_kernel,
        grid=(m // tm,),
        in_specs=[pl.BlockSpec((tm, d), lambda i: (i, 0)), _resident((d, 2 * dff)), _resident((dff, d)),
                  _resident((1, d)), _resident((1, d))],
        out_specs=pl.BlockSpec((tm, d), lambda i: (i, 0)),
        out_shape=jax.ShapeDtypeStruct((m, d), F32),
        compiler_params=_params(1),
        name="ffn",
    )(x, win_bf, wout_bf, g.reshape(1, d), b.reshape(1, d))


Q_SCALE = HEAD_DIM ** -0.5 * LOG2E
W_SCALE = IDX_HEADS ** -0.5 * IDX_DIM ** -0.5
QI_PAD = IDX_HEADS * LANES


def _split_attn_weights(w_in):
    d = D_MODEL
    o3 = 3 * d
    o4 = o3 + IDX_HEADS * IDX_DIM
    o5 = o4 + IDX_DIM
    wq, wk, wv = w_in[:, :d], w_in[:, d:2 * d], w_in[:, 2 * d:o3]
    wqi = w_in[:, o3:o4].reshape(d, IDX_HEADS, IDX_DIM)
    wqi = jnp.pad(wqi, ((0, 0), (0, 0), (0, LANES - IDX_DIM))).reshape(d, QI_PAD)
    wkw = jnp.pad(w_in[:, o4:], ((0, 0), (0, LANES - IDX_DIM - IDX_HEADS)))
    return tuple(a.astype(BF16) for a in (wq, wk, wv, wqi, wkw))


def _attn_proj_t_kernel(x_ref, wk_ref, wv_ref, wkw_ref, wqT_ref, wvT_ref, wqiT_ref, wwT_ref,
                        k32_ref, v32_ref, kiw_ref, qT_ref, kbf_ref, vT_ref, qiT_ref, kip_ref, wT_ref):
    xb = x_ref[...].astype(BF16)
    k = _mm(xb, wk_ref[...])
    k32_ref[...] = k
    kbf_ref[...] = k.astype(BF16)
    v32_ref[...] = _mm(xb, wv_ref[...])
    kiw = _mm(xb, wkw_ref[...])
    kiw_ref[...] = kiw
    lane = lax.broadcasted_iota(I32, kiw.shape, 1)
    kip_ref[...] = jnp.where(lane < IDX_DIM, kiw, 0.0).astype(BF16)
    qT_ref[...] = (_mm_nt(wqT_ref[...], xb) * Q_SCALE).astype(BF16)
    vT_ref[...] = _mm_nt(wvT_ref[...], xb).astype(BF16)
    qiT_ref[...] = _mm_nt(wqiT_ref[...], xb).astype(BF16)
    wT_ref[...] = _mm_nt(wwT_ref[...], xb) * W_SCALE


def attn_proj_t(x, w_in):
    m, d = x.shape
    tm = _row_tile(m)
    wq, wk, wv, wqi, wkw = _split_attn_weights(w_in)
    o5 = 3 * d + IDX_HEADS * IDX_DIM + IDX_DIM
    wwT = jnp.pad(w_in[:, o5:].T, ((0, 16 - IDX_HEADS), (0, 0))).astype(BF16)
    row = lambda n: pl.BlockSpec((tm, n), lambda i: (i, 0))
    col = lambda n: pl.BlockSpec((n, tm), lambda i: (0, i))
    return pl.pallas_call(
        _attn_proj_t_kernel,
        grid=(m // tm,),
        in_specs=[row(d), _resident((d, d)), _resident((d, d)), _resident((d, LANES)),
                  _resident((d, d)), _resident((d, d)), _resident((QI_PAD, d)), _resident((16, d))],
        out_specs=[row(d), row(d), row(LANES), col(d), row(d), col(d), col(QI_PAD), row(LANES), col(16)],
        out_shape=[jax.ShapeDtypeStruct((m, d), F32), jax.ShapeDtypeStruct((m, d), F32),
                   jax.ShapeDtypeStruct((m, LANES), F32), jax.ShapeDtypeStruct((d, m), BF16),
                   jax.ShapeDtypeStruct((m, d), BF16), jax.ShapeDtypeStruct((d, m), BF16),
                   jax.ShapeDtypeStruct((QI_PAD, m), BF16), jax.ShapeDtypeStruct((m, LANES), BF16),
                   jax.ShapeDtypeStruct((16, m), F32)],
        compiler_params=_params(1),
        name="attn_proj_t",
    )(x, wk, wv, wkw, wq.T, wv.T, wqi.T, wwT)


def _attn_proj_s_kernel(x_ref, wq_ref, wk_ref, wv_ref, wqi_ref, wkw_ref,
                        q_ref, k32_ref, v32_ref, qi_ref, kiw_ref):
    xb = x_ref[...].astype(BF16)
    q_ref[...] = (_mm(xb, wq_ref[...]) * Q_SCALE).astype(BF16)
    k32_ref[...] = _mm(xb, wk_ref[...])
    v32_ref[...] = _mm(xb, wv_ref[...])
    qi_ref[...] = _mm(xb, wqi_ref[...]).astype(BF16)
    kiw_ref[...] = _mm(xb, wkw_ref[...])


def attn_proj_s(x, w_in):
    m, d = x.shape
    tm = _row_tile(m)
    wq, wk, wv, wqi, wkw = _split_attn_weights(w_in)
    row = lambda n: pl.BlockSpec((tm, n), lambda i: (i, 0))
    return pl.pallas_call(
        _attn_proj_s_kernel,
        grid=(m // tm,),
        in_specs=[row(d), _resident((d, d)), _resident((d, d)), _resident((d, d)),
                  _resident((d, QI_PAD)), _resident((d, LANES))],
        out_specs=[row(d), row(d), row(d), row(QI_PAD), row(LANES)],
        out_shape=[jax.ShapeDtypeStruct((m, d), BF16), jax.ShapeDtypeStruct((m, d), F32),
                   jax.ShapeDtypeStruct((m, d), F32), jax.ShapeDtypeStruct((m, QI_PAD), BF16),
                   jax.ShapeDtypeStruct((m, LANES), F32)],
        compiler_params=_params(1),
        name="attn_proj_s",
    )(x, wq, wk, wv, wqi, wkw)


def _prompt_schedule(t):
    qb_of, kt_of, fl = [], [], []
    for qb in range(t // BQ):
        nb = -(-max(qb * BQ - CH, 0) // TK)
        qb_of.append(qb); kt_of.append(0); fl.append(1 | (2 if nb == 0 else 0))
        for kt in range(nb):
            qb_of.append(qb); kt_of.append(kt); fl.append(2 if kt == nb - 1 else 0)
    return (np.asarray(qb_of, np.int32), np.asarray(kt_of, np.int32), np.asarray(fl, np.int32))


def _attn_prompt_kernel(qb_ref, kt_ref, fl_ref,
                        qT_ref, qiT_ref, wT_ref, kip_ref, kb_ref, vTb_ref, kp_ref, kd_ref, vTp_ref, vTd_ref,
                        rb_ref, o_ref,
                        s_ref, thr_ref, bb_ref, m_ref, l_ref, acc_ref, madd_ref, *, k_sel, idx_bits):
    step = pl.program_id(0)
    qb = qb_ref[step]
    kt = kt_ref[step]
    flags = fl_ref[step]
    is_band = (flags & 1) != 0
    is_last = (flags & 2) != 0
    q0 = qb * BQ
    nch = 2 * (qb + 1)
    cband = 2 * qb - 1
    kf = float(k_sel)

    @pl.when(step == 0)
    def _build_band_bias():
        r = lax.broadcasted_iota(I32, (BAND, BQ), 0)
        j = lax.broadcasted_iota(I32, (BAND, BQ), 1)
        dist = CH + j - r
        for h in range(N_HEADS):
            bb_ref[h] = _bias_from_dist(dist, rb_ref, h)

    def attend(h, kh, vth, madd, bias, cbias):
        hs = slice(h * HEAD_DIM, (h + 1) * HEAD_DIM)
        lt = _mm(kh, qT_ref[hs, :])
        if bias is not None:
            lt = lt + bias
        lt = lt + madd
        mx = jnp.max(lt, axis=0, keepdims=True) + cbias
        m_old = m_ref[h:h + 1, :]
        m_new = jnp.maximum(m_old, mx)
        alpha = jnp.exp2(m_old - m_new)
        p = jnp.exp2(lt - (m_new - cbias))
        psum = p[0:SUBLANES]
        for i in range(1, p.shape[0] // SUBLANES):
            psum = psum + p[i * SUBLANES:(i + 1) * SUBLANES]
        l_ref[h] = alpha * l_ref[h] + psum
        acc_ref[hs, :] = alpha * acc_ref[hs, :] + _mm(vth, p.astype(BF16))
        m_ref[h:h + 1, :] = m_new

    @pl.when(is_band)
    def _band_step():
        def idx_body(gi, carry):
            rows = IDX_GROUP * CH
            kic = kip_ref[pl.ds(pl.multiple_of(gi * rows, rows), rows), :]
            acc = None
            for h in range(IDX_HEADS):
                sc = _mm(kic, qiT_ref[h * LANES:(h + 1) * LANES, :])
                val = jnp.maximum(sc, 0.0) * wT_ref[h:h + 1, :]
                acc = val if acc is None else acc + val
            for i in range(IDX_GROUP):
                s_ref[gi * IDX_GROUP + i] = acc[i * CH:(i + 1) * CH]
            return carry

        lax.fori_loop(0, (nch + IDX_GROUP - 1) // IDX_GROUP, idx_body, 0)
        rr = lax.broadcasted_iota(I32, (CH, BQ), 0)
        jj = lax.broadcasted_iota(I32, (CH, BQ), 1)
        for i in range(BQ // CH):
            c = 2 * qb + i
            s_ref[c] = jnp.where(i * CH + rr <= jj, s_ref[c], NEG_INF)

        def count(pred):
            def body(i, acc):
                for c in (2 * i, 2 * i + 1):
                    ind = jnp.where(pred(c, s_ref[c]), 1.0, 0.0)
                    for r in range(CH // SUBLANES):
                        acc = acc + ind[r * SUBLANES:(r + 1) * SUBLANES]
                return acc
            acc = lax.fori_loop(0, qb + 1, body, jnp.zeros((SUBLANES, BQ), F32))
            return jnp.broadcast_to(jnp.sum(acc, axis=0, keepdims=True), (SUBLANES, BQ))

        def count_ge(cand_key):
            cand = _key_to_f32(cand_key)[0:1, :]
            return count(lambda c, x: x >= cand)

        total = (nch * CH).astype(F32)
        c0 = count(lambda c, x: x >= 0.0)
        ans = jnp.where(c0 >= kf, 0, INT_MIN).astype(I32)
        cnt = jnp.where(c0 >= kf, c0, total)

        def bit_cond(carry):
            i, _, cnt = carry
            return jnp.logical_and(i < 31, jnp.max(jnp.abs(cnt - kf)) > 0.5)

        def bit_body(carry):
            i, ans, cnt = carry
            cand = ans + jnp.left_shift(jnp.int32(1), 30 - i)
            c = count_ge(cand)
            ok = jnp.logical_and(c >= kf, cnt != kf)
            return i + 1, jnp.where(ok, cand, ans), jnp.where(ok, c, cnt)

        _, ans, cnt = lax.while_loop(bit_cond, bit_body, (jnp.int32(0), ans, cnt))
        none = ans == INT_MIN
        thr = jnp.where(none, NEG_INF, _key_to_f32(ans))
        thr_ref[...] = thr

        has = jnp.logical_and(cnt > kf, jnp.logical_not(none))
        any_ties = jnp.max(jnp.where(has, 1.0, 0.0)) > 0.5

        @pl.when(any_ties)
        def _resolve_ties():
            t1 = thr[0:1, :]
            n_gt = count_ge(ans + 1)
            need = jnp.where(has, kf - n_gt, 3.0e38)
            rows = lax.broadcasted_iota(I32, (CH, BQ), 0)

            def j_body(i, j):
                cj = j + jnp.left_shift(jnp.int32(1), idx_bits - 1 - i)
                c1 = cj[0:1, :]
                f = count(lambda c, x: jnp.logical_and(x == t1, c * CH + rows < c1))
                return jnp.where(f < need, cj, j)

            jstar = lax.fori_loop(0, idx_bits, j_body, jnp.zeros((SUBLANES, BQ), I32))
            j1 = jstar[0:1, :]
            h1 = has[0:1, :]

            def demote(c, carry):
                x = s_ref[c]
                drop = jnp.logical_and(jnp.logical_and(x == t1, c * CH + rows > j1), h1)
                s_ref[c] = jnp.where(drop, NEG_INF, x)
                return carry

            lax.fori_loop(0, nch, demote, 0)

        m_ref[...] = jnp.full(m_ref.shape, M_FLOOR, F32)
        l_ref[...] = jnp.zeros(l_ref.shape, F32)
        acc_ref[...] = jnp.zeros(acc_ref.shape, F32)
        thr1 = thr_ref[0:1, :]
        cprev = jnp.maximum(cband, 0)
        sc = jnp.concatenate([s_ref[cprev], s_ref[2 * qb], s_ref[2 * qb + 1]], axis=0)
        r = lax.broadcasted_iota(I32, (BAND, BQ), 0)
        j = lax.broadcasted_iota(I32, (BAND, BQ), 1)
        ok = jnp.logical_and(sc >= thr1, r - CH <= j)
        ok = jnp.logical_and(ok, jnp.logical_or(r >= CH, qb > 0))
        madd = jnp.where(ok, 0.0, NEG_INF)
        for h in range(N_HEADS):
            hs = slice(h * HEAD_DIM, (h + 1) * HEAD_DIM)
            kh = jnp.concatenate([kp_ref[:, hs], kd_ref[:, hs]], axis=0)
            vth = jnp.concatenate([vTp_ref[hs, :], vTd_ref[hs, :]], axis=1)
            attend(h, kh, vth, madd, bb_ref[h], 0.0)

    @pl.when(jnp.logical_not(is_band))
    def _bulk_step():
        thr1 = thr_ref[0:1, :]
        for i in range(TK // CH):
            c = kt * (TK // CH) + i
            x = s_ref[jnp.minimum(c, cband - 1)]
            sel = jnp.where(x >= thr1, 0.0, NEG_INF)
            madd_ref[i * CH:(i + 1) * CH, :] = jnp.where(c < cband, sel, NEG_INF)
        for h in range(N_HEADS):
            hs = slice(h * HEAD_DIM, (h + 1) * HEAD_DIM)
            attend(h, kb_ref[:, hs], vTb_ref[hs, :], madd_ref[...], None,
                   rb_ref[REL_BUCKETS - 1, h] * LOG2E)

    @pl.when(is_last)
    def _finalize():
        for h in range(N_HEADS):
            hs = slice(h * HEAD_DIM, (h + 1) * HEAD_DIM)
            lsum = jnp.sum(l_ref[h], axis=0, keepdims=True)
            o_ref[:, hs] = (acc_ref[hs, :] / lsum).T.astype(BF16)


def attn_prompt(qT, qiT, wT, kip, kbf, vT, rel_bias):
    d, t = qT.shape
    k_sel = min(TOPK_MAX, t // 4)
    qb_of, kt_of, fl = _prompt_schedule(t)
    n_steps = len(qb_of)
    tk = min(TK, t)
    qcol = lambda n: pl.BlockSpec((n, BQ), lambda s, qb, kt, fl: (0, qb[s]))
    in_specs = [
        qcol(d), qcol(QI_PAD), qcol(16),
        pl.BlockSpec((t, LANES), lambda s, qb, kt, fl: (0, 0), pipeline_mode=pl.Buffered(1)),
        pl.BlockSpec((tk, d), lambda s, qb, kt, fl: (kt[s], 0)),
        pl.BlockSpec((d, tk), lambda s, qb, kt, fl: (0, kt[s])),
        pl.BlockSpec((CH, d), lambda s, qb, kt, fl: (jnp.maximum(2 * qb[s] - 1, 0), 0)),
        pl.BlockSpec((BQ, d), lambda s, qb, kt, fl: (qb[s], 0)),
        pl.BlockSpec((d, CH), lambda s, qb, kt, fl: (0, jnp.maximum(2 * qb[s] - 1, 0))),
        pl.BlockSpec((d, BQ), lambda s, qb, kt, fl: (0, qb[s])),
        pl.BlockSpec(memory_space=pltpu.SMEM),
    ]
    grid_spec = pltpu.PrefetchScalarGridSpec(
        num_scalar_prefetch=3,
        grid=(n_steps,),
        in_specs=in_specs,
        out_specs=pl.BlockSpec((BQ, d), lambda s, qb, kt, fl: (qb[s], 0)),
        scratch_shapes=[
            pltpu.VMEM((t // CH, CH, BQ), F32),
            pltpu.VMEM((SUBLANES, BQ), F32),
            pltpu.VMEM((N_HEADS, BAND, BQ), F32),
            pltpu.VMEM((N_HEADS, BQ), F32),
            pltpu.VMEM((N_HEADS, SUBLANES, BQ), F32),
            pltpu.VMEM((d, BQ), F32),
            pltpu.VMEM((tk, BQ), F32),
        ],
    )
    kern = functools.partial(_attn_prompt_kernel, k_sel=k_sel, idx_bits=max(t.bit_length() - 1, 1))
    return pl.pallas_call(
        kern,
        grid_spec=grid_spec,
        out_shape=jax.ShapeDtypeStruct((t, d), BF16),
        compiler_params=_params(1),
        name="attn_prompt",
    )(jnp.asarray(qb_of), jnp.asarray(kt_of), jnp.asarray(fl),
      qT, qiT, wT, kip, kbf, vT, kbf, kbf, vT, vT, rel_bias)


def _sample_scores_kernel(pt_ref, qi_ref, w_ref, *rest, n_groups):
    pages = rest[:S1_PAGES]
    kin_ref, o_ref, kbuf_ref = rest[S1_PAGES:]
    b = pl.program_id(0)
    g = pl.program_id(1)
    t = o_ref.shape[1]

    @pl.when(jnp.logical_and(b == 0, g == 0))
    def _zero_pad_rows():
        kbuf_ref[...] = jnp.zeros(kbuf_ref.shape, BF16)

    def head_sum(val):
        s = val[0:t]
        for h in range(1, IDX_HEADS):
            s = s + val[h * t:(h + 1) * t]
        return s

    wcol = w_ref[:, 0:1]

    @pl.when(g < n_groups)
    def _past_pages():
        for i in range(S1_PAGES):
            kbuf_ref[0:IDX_DIM, i * PAGE_SIZE:(i + 1) * PAGE_SIZE] = pages[i][...].astype(BF16)
        s = head_sum(jnp.maximum(_mm(qi_ref[...], kbuf_ref[...]), 0.0) * wcol)
        for i in range(S1_PAGES):
            o_ref[i] = s[:, i * PAGE_SIZE:(i + 1) * PAGE_SIZE]

    @pl.when(g == n_groups)
    def _new_keys():
        s = head_sum(jnp.maximum(_mm_nt(qi_ref[...], kin_ref[...]), 0.0) * wcol)
        row = lax.broadcasted_iota(I32, s.shape, 0)
        lane = lax.broadcasted_iota(I32, s.shape, 1)
        o_ref[0] = jnp.where(lane <= row, s, NEG_INF)
        for i in range(1, S1_PAGES):
            o_ref[i] = jnp.full(s.shape, NEG_INF, F32)


def sample_scores(page_table, qi_ht, w_ht, cache_kidx, layer, ki_new):
    b, n_pages = page_table.shape
    t = qi_ht.shape[1] // IDX_HEADS
    n_groups = n_pages // S1_PAGES
    page_spec = lambda i: pl.BlockSpec(
        (None, None, IDX_DIM, PAGE_SIZE),
        lambda bi, g, pt: (layer, pt[bi, jnp.minimum(g, n_groups - 1) * S1_PAGES + i], 0, 0))
    grid_spec = pltpu.PrefetchScalarGridSpec(
        num_scalar_prefetch=1,
        grid=(b, n_groups + 1),
        in_specs=[pl.BlockSpec((None, IDX_HEADS * t, LANES), lambda bi, g, pt: (bi, 0, 0)),
                  pl.BlockSpec((None, IDX_HEADS * t, LANES), lambda bi, g, pt: (bi, 0, 0))]
                 + [page_spec(i) for i in range(S1_PAGES)]
                 + [pl.BlockSpec((None, PAGE_SIZE, LANES), lambda bi, g, pt: (bi, 0, 0))],
        out_specs=pl.BlockSpec((None, S1_PAGES, t, LANES), lambda bi, g, pt: (bi, g, 0, 0)),
        scratch_shapes=[pltpu.VMEM((LANES, S1_PAGES * PAGE_SIZE), BF16)],
    )
    kidx_t = jnp.swapaxes(cache_kidx, 2, 3)
    return pl.pallas_call(
        functools.partial(_sample_scores_kernel, n_groups=n_groups),
        grid_spec=grid_spec,
        out_shape=jax.ShapeDtypeStruct((b, (n_groups + 1) * S1_PAGES, t, LANES), F32),
        compiler_params=_params(2),
        name="sample_scores",
    )(page_table, qi_ht, w_ht, *([kidx_t] * S1_PAGES), ki_new)


def _sample_select_kernel(s_ref, o_ref, *, k_sel, past, idx_bits):
    x = s_ref[...]
    nck, t, _ = x.shape
    kf = float(k_sel)
    cidx = lax.broadcasted_iota(I32, x.shape, 0)
    row = lax.broadcasted_iota(I32, x.shape, 1)
    lane = lax.broadcasted_iota(I32, x.shape, 2)
    idx = cidx * LANES + lane

    def count(pred):
        per_lane = jnp.sum(jnp.where(pred, 1.0, 0.0), axis=0)
        return jnp.broadcast_to(jnp.sum(per_lane, axis=1, keepdims=True), (t, LANES))

    def count_ge(cand_key):
        return count(x >= _key_to_f32(cand_key)[None])

    total = float(nck * LANES)
    c0 = count_ge(jnp.zeros((t, LANES), I32))
    ans = jnp.where(c0 >= kf, 0, INT_MIN).astype(I32)
    cnt = jnp.where(c0 >= kf, c0, total)

    def bit_cond(carry):
        i, _, cnt = carry
        return jnp.logical_and(i < 31, jnp.max(jnp.abs(cnt - kf)) > 0.5)

    def bit_body(carry):
        i, ans, cnt = carry
        cand = ans + jnp.left_shift(jnp.int32(1), 30 - i)
        c = count_ge(cand)
        ok = jnp.logical_and(c >= kf, cnt != kf)
        return i + 1, jnp.where(ok, cand, ans), jnp.where(ok, c, cnt)

    _, ans, cnt = lax.while_loop(bit_cond, bit_body, (jnp.int32(0), ans, cnt))
    none = ans == INT_MIN
    thr = jnp.where(none, NEG_INF, _key_to_f32(ans))
    has = jnp.logical_and(cnt > kf, jnp.logical_not(none))
    any_ties = jnp.max(jnp.where(has, 1.0, 0.0)) > 0.5
    new = idx - past
    valid = jnp.logical_or(idx < past, jnp.logical_and(new < t, new <= row))
    keep = jnp.logical_and(x >= thr[None], valid)

    @pl.when(jnp.logical_not(any_ties))
    def _no_ties():
        o_ref[...] = jnp.where(keep, 0.0, NEG_INF)

    @pl.when(any_ties)
    def _resolve_ties():
        n_gt = count_ge(ans + 1)
        need = jnp.where(has, kf - n_gt, 3.0e38)
        eq = x == thr[None]

        def j_body(i, j):
            cj = j + jnp.left_shift(jnp.int32(1), idx_bits - 1 - i)
            f = count(jnp.logical_and(eq, idx < cj[None]))
            return jnp.where(f < need, cj, j)

        jstar = lax.fori_loop(0, idx_bits, j_body, jnp.zeros((t, LANES), I32))
        drop = jnp.logical_and(jnp.logical_and(eq, idx > jstar[None]), has[None])
        o_ref[...] = jnp.where(jnp.logical_and(keep, jnp.logical_not(drop)), 0.0, NEG_INF)


def sample_select(scores, past, k_sel):
    b, nck, t, _ = scores.shape
    spec = pl.BlockSpec((None, nck, t, LANES), lambda i: (i, 0, 0, 0))
    kern = functools.partial(_sample_select_kernel, k_sel=k_sel, past=past,
                             idx_bits=(nck * LANES - 1).bit_length())
    return pl.pallas_call(
        kern, grid=(b,), in_specs=[spec], out_specs=spec,
        out_shape=jax.ShapeDtypeStruct(scores.shape, F32),
        compiler_params=_params(1),
        name="sample_select",
    )(scores)


def _sample_attn_kernel(pt_ref, q_ref, madd_ref, *rest, n_groups, t):
    kpages = rest[:S3_PAGES]
    vpages = rest[S3_PAGES:2 * S3_PAGES]
    (kn_ref, vn_ref, rb_ref, o_ref,
     kcat_ref, vcat_ref, m_ref, l_ref, acc_ref, bconst_ref, blast_ref, bnew_ref) = rest[2 * S3_PAGES:]
    b = pl.program_id(0)
    g = pl.program_id(1)
    rows = N_HEADS * t

    @pl.when(jnp.logical_and(b == 0, g == 0))
    def _build_bias_tables():
        tq = lax.broadcasted_iota(I32, (t, LANES), 0)
        ln = lax.broadcasted_iota(I32, (t, LANES), 1)
        for h in range(N_HEADS):
            hs = slice(h * t, (h + 1) * t)
            bconst_ref[hs, :] = jnp.full((t, LANES), rb_ref[REL_BUCKETS - 1, h] * LOG2E, F32)
            blast_ref[hs, :] = _bias_from_dist(PAGE_SIZE + tq - ln, rb_ref, h)
            bnew_ref[hs, :] = _bias_from_dist(tq - ln, rb_ref, h)

    @pl.when(g == 0)
    def _init():
        m_ref[...] = jnp.full(m_ref.shape, M_FLOOR, F32)
        l_ref[...] = jnp.zeros(l_ref.shape, F32)
        acc_ref[...] = jnp.zeros(acc_ref.shape, F32)

    def tile_heads(m8):
        return jnp.concatenate([m8] * N_HEADS, axis=0)

    def attend(kk, vv, bias, madd):
        lg = _mm_nt(q_ref[...], kk) + bias + madd
        mx = jnp.max(lg, axis=1, keepdims=True)
        m_old = m_ref[...]
        m_new = jnp.maximum(m_old, mx)
        alpha = jnp.exp2(m_old - m_new)
        p = jnp.exp2(lg - m_new[:, 0:1])
        l_ref[...] = alpha * l_ref[...] + jnp.sum(p, axis=1, keepdims=True)
        acc_ref[...] = alpha[:, 0:1] * acc_ref[...] + _mm(p.astype(BF16), vv)
        m_ref[...] = m_new

    @pl.when(g < n_groups)
    def _cached_pages():
        for i in range(S3_PAGES):
            rs = slice(i * PAGE_SIZE, (i + 1) * PAGE_SIZE)
            for h in range(N_HEADS):
                hs = slice(h * HEAD_DIM, (h + 1) * HEAD_DIM)
                head_rows = pl.ds(h, PAGE_SIZE, stride=N_HEADS)
                kcat_ref[rs, hs] = kpages[i][head_rows, :].astype(BF16)
                vcat_ref[rs, hs] = vpages[i][head_rows, :].astype(BF16)
        last = jnp.where(g == n_groups - 1, blast_ref[...], bconst_ref[...])
        bias = jnp.concatenate([bconst_ref[...]] * (S3_PAGES - 1) + [last], axis=1)
        madd = jnp.concatenate([tile_heads(madd_ref[i]) for i in range(S3_PAGES)], axis=1)
        attend(kcat_ref[...], vcat_ref[...], bias, madd)

    @pl.when(g == n_groups)
    def _new_keys_and_finish():
        attend(kn_ref[...], vn_ref[...], bnew_ref[...], tile_heads(madd_ref[0]))
        for h in range(N_HEADS):
            hs = slice(h * HEAD_DIM, (h + 1) * HEAD_DIM)
            rs = slice(h * t, (h + 1) * t)
            o_ref[:, hs] = acc_ref[rs, hs] / l_ref[rs, 0:1]


def sample_attn(page_table, q_bd, madd, cache_k, cache_v, layer, k_new, v_new, rel_bias):
    b, n_pages = page_table.shape
    rows, d = q_bd.shape[1:]
    t = rows // N_HEADS
    n_groups = n_pages // S3_PAGES
    page_spec = lambda i: pl.BlockSpec(
        (None, None, PAGE_SIZE * N_HEADS, HEAD_DIM),
        lambda bi, g, pt: (layer, pt[bi, jnp.minimum(g, n_groups - 1) * S3_PAGES + i], 0, 0))
    per_b = lambda shape: pl.BlockSpec((None,) + shape, lambda bi, g, pt: (bi,) + (0,) * len(shape))
    grid_spec = pltpu.PrefetchScalarGridSpec(
        num_scalar_prefetch=1,
        grid=(b, n_groups + 1),
        in_specs=[per_b((rows, d)),
                  pl.BlockSpec((None, S3_PAGES, t, LANES), lambda bi, g, pt: (bi, g, 0, 0))]
                 + [page_spec(i) for i in range(S3_PAGES)] * 2
                 + [per_b((PAGE_SIZE, d)), per_b((PAGE_SIZE, d)), pl.BlockSpec(memory_space=pltpu.SMEM)],
        out_specs=per_b((t, d)),
        scratch_shapes=[pltpu.VMEM((S3_PAGES * PAGE_SIZE, d), BF16), pltpu.VMEM((S3_PAGES * PAGE_SIZE, d), BF16),
                        pltpu.VMEM((rows, LANES), F32), pltpu.VMEM((rows, LANES), F32),
                        pltpu.VMEM((rows, d), F32),
                        pltpu.VMEM((rows, LANES), F32), pltpu.VMEM((rows, LANES), F32),
                        pltpu.VMEM((rows, LANES), F32)],
    )
    return pl.pallas_call(
        functools.partial(_sample_attn_kernel, n_groups=n_groups, t=t),
        grid_spec=grid_spec,
        out_shape=jax.ShapeDtypeStruct((b, t, d), F32),
        compiler_params=_params(2),
        name="sample_attn",
    )(page_table, q_bd, madd, *([cache_k] * S3_PAGES), *([cache_v] * S3_PAGES), k_new, v_new, rel_bias)


def attn_sample(x, cache_k, cache_v, cache_kidx, page_table, layer, w_in, rel_bias):
    b, t, d = x.shape
    past = page_table.shape[1] * PAGE_SIZE
    k_sel = min(TOPK_MAX, (past + t) // 4)
    q, k32, v32, qi, kiw = attn_proj_s(x.reshape(b * t, d), w_in)
    qi_ht = qi.reshape(b, t, IDX_HEADS, LANES).transpose(0, 2, 1, 3).reshape(b, IDX_HEADS * t, LANES)
    wi = kiw[:, IDX_DIM:IDX_DIM + IDX_HEADS] * W_SCALE
    w_ht = jnp.broadcast_to(wi.reshape(b, t, IDX_HEADS).transpose(0, 2, 1).reshape(b, IDX_HEADS * t, 1),
                            (b, IDX_HEADS * t, LANES))
    head_of_col = jnp.arange(d, dtype=I32) // HEAD_DIM
    q_bd = jnp.where(head_of_col[None, None, None, :] == jnp.arange(N_HEADS, dtype=I32)[None, :, None, None],
                     q.reshape(b, 1, t, d), jnp.zeros((), BF16)).reshape(b, N_HEADS * t, d)
    pad_rows = lambda a: jnp.pad(a.reshape(b, t, -1), ((0, 0), (0, PAGE_SIZE - t), (0, 0))).astype(BF16)
    ki_new = pad_rows(jnp.pad(kiw[:, :IDX_DIM], ((0, 0), (0, LANES - IDX_DIM))))
    scores = sample_scores(page_table, qi_ht, w_ht, cache_kidx, layer, ki_new)
    madd = sample_select(scores, past, k_sel)
    page_rows = lambda c: c.reshape(c.shape[0], c.shape[1], PAGE_SIZE * N_HEADS, HEAD_DIM)
    out = sample_attn(page_table, q_bd, madd, page_rows(cache_k), page_rows(cache_v), layer,
                      pad_rows(k32), pad_rows(v32), rel_bias)
    return out.reshape(b * t, d), k32, v32, kiw[:, :IDX_DIM]


def kernel(x_prompt, x_sample, state_conv, cache_k, cache_v, cache_kidx, page_table, rel_bias,
           w_pw1, b_pw1, w_dw, b_dw, conv_norm_g, conv_norm_b, w_pw2, b_pw2,
           w_attn_in, w_attn_out, w_ffn_in, w_ffn_out, ln_mix_g, ln_mix_b, ln_ffn_g, ln_ffn_b):
    bp, tp, d = x_prompt.shape
    bs, ts, _ = x_sample.shape
    assert bp == 1 and d == D_MODEL and tp % TK == 0
    xp = x_prompt.reshape(tp, d)
    xs = x_sample.reshape(bs * ts, d)
    bf = lambda a: a.astype(BF16)

    w1, w2 = bf(w_pw1[0]), bf(w_pw2[0])
    up = glu(xp, w1, b_pw1[0])
    us = glu(xs, w1, b_pw1[0])
    zp = conv_prompt(up, w_dw[0], b_dw[0], conv_norm_g[0], conv_norm_b[0])
    zs, conv_s = conv_sample(us.reshape(bs, ts, d), state_conv[0], w_dw[0], b_dw[0],
                             conv_norm_g[0], conv_norm_b[0])
    conv_p = up[tp - CONV_STATE:]
    xp = proj_res_ln(zp, w2, b_pw2[0], xp, ln_mix_g[0], ln_mix_b[0])
    xs = proj_res_ln(zs.reshape(bs * ts, d), w2, b_pw2[0], xs, ln_mix_g[0], ln_mix_b[0])
    wf_in, wf_out = bf(w_ffn_in[0]), bf(w_ffn_out[0])
    xp = ffn(xp, wf_in, wf_out, ln_ffn_g[0], ln_ffn_b[0])
    xs = ffn(xs, wf_in, wf_out, ln_ffn_g[0], ln_ffn_b[0])

    w_o = bf(w_attn_out[0])
    no_bias = jnp.zeros((d,), F32)
    kp32, vp32, kiwp, qT, kbf, vT, qiT, kip, wT = attn_proj_t(xp, w_attn_in[0])
    ap = attn_prompt(qT, qiT, wT, kip, kbf, vT, rel_bias)
    xp = proj_res_ln(ap, w_o, no_bias, xp, ln_mix_g[1], ln_mix_b[1])
    a_s, ks32, vs32, kis = attn_sample(xs.reshape(bs, ts, d), cache_k, cache_v, cache_kidx, page_table, 0,
                                       w_attn_in[0], rel_bias)
    xs = proj_res_ln(a_s, w_o, no_bias, xs, ln_mix_g[1], ln_mix_b[1])
    wf_in, wf_out = bf(w_ffn_in[1]), bf(w_ffn_out[1])
    xp = ffn(xp, wf_in, wf_out, ln_ffn_g[1], ln_ffn_b[1])
    xs = ffn(xs, wf_in, wf_out, ln_ffn_g[1], ln_ffn_b[1])

    return (xp.reshape(1, tp, d), xs.reshape(bs, ts, d),
            conv_p.reshape(1, 1, CONV_STATE, d), conv_s.reshape(1, bs, CONV_STATE, d),
            kp32.reshape(1, 1, tp, N_HEADS, HEAD_DIM), vp32.reshape(1, 1, tp, N_HEADS, HEAD_DIM),
            kiwp[:, :IDX_DIM].reshape(1, 1, tp, IDX_DIM),
            ks32.reshape(1, bs, ts, N_HEADS, HEAD_DIM), vs32.reshape(1, bs, ts, N_HEADS, HEAD_DIM),
            kis.reshape(1, bs, ts, IDX_DIM))
```

```python
import functools
import math

import numpy as np
import jax
import jax.numpy as jnp
from jax import lax
from jax.experimental import pallas as pl
from jax.experimental.pallas import tpu as pltpu

F32, BF16, I32 = jnp.float32, jnp.bfloat16, jnp.int32

D_MODEL = 1024
N_HEADS = 8
HEAD_DIM = D_MODEL // N_HEADS
IDX_HEADS = 8
IDX_DIM = 64
TOPK_MAX = 256
CONV_WIDTH = 31
CONV_STATE = CONV_WIDTH - 1
D_FF = 2816
REL_BUCKETS = 32
REL_MAX_DIST = 128
PAGE_SIZE = 128
DEPTH = 2
ALPHA = (2 * DEPTH) ** 0.25
LN_EPS = 1e-5

LANES = 128
SUBLANES = 8
V7X_VMEM_LIMIT_BYTES = 56 * 1024 * 1024

LOG2E = math.log2(math.e)
NEG_INF = float("-inf")
M_FLOOR = -3.0e38
INT_MIN = -(2 ** 31)

BQ = 256
CH = 128
TK = 1024
IDX_GROUP = 4
QK_ROWS = 128
PV_KEYS = 256
BAND = CH + BQ

S1_PAGES = 16
S3_PAGES = 8


def _bucket_lower_bounds():
    max_exact = REL_BUCKETS // 2
    lows = [None] * REL_BUCKETS
    for d in range(0, REL_MAX_DIST + 1):
        if d < max_exact:
            b = d
        else:
            b = max_exact + int(math.log(d / max_exact) / math.log(REL_MAX_DIST / max_exact)
                                * (REL_BUCKETS - max_exact))
            b = min(b, REL_BUCKETS - 1)
        if lows[b] is None:
            lows[b] = d
    nxt = REL_MAX_DIST
    for b in range(REL_BUCKETS - 1, -1, -1):
        if lows[b] is None:
            lows[b] = nxt
        nxt = lows[b]
    return lows


BUCKET_LOW = _bucket_lower_bounds()


def _mm(a, b):
    return jnp.dot(a, b, preferred_element_type=F32)


def _mm_nt(a, b):
    return lax.dot_general(a, b, (((1,), (1,)), ((), ())), preferred_element_type=F32)


def _layer_norm(y, g, b):
    mu = jnp.mean(y, axis=-1, keepdims=True)
    d = y - mu
    var = jnp.mean(d * d, axis=-1, keepdims=True)
    return d * lax.rsqrt(var + LN_EPS) * g + b


def _params(n_axes):
    return pltpu.CompilerParams(dimension_semantics=("arbitrary",) * n_axes,
                                vmem_limit_bytes=V7X_VMEM_LIMIT_BYTES)


def _resident(shape):
    nd = len(shape)
    return pl.BlockSpec(shape, lambda *_: (0,) * nd, pipeline_mode=pl.Buffered(1))


def _row_tile(m):
    for t in (512, 256, 128, 64, 32, 16, 8):
        if m % t == 0:
            return t
    raise ValueError(f"row count {m} is not a multiple of 8")


def _bias_from_dist(dist, rb_ref, h):
    bias = jnp.full(dist.shape, rb_ref[0, h] * LOG2E, F32)
    for b in range(1, REL_BUCKETS):
        bias = jnp.where(dist >= BUCKET_LOW[b], rb_ref[b, h] * LOG2E, bias)
    return bias


def _key_to_f32(k):
    bits = jnp.where(k < 0, k ^ jnp.int32(0x7FFFFFFF), k)
    return lax.bitcast_convert_type(bits, F32)


def _glu_kernel(x_ref, w_ref, b_ref, u_ref):
    d = u_ref.shape[-1]
    xb = x_ref[...].astype(BF16)
    a = _mm(xb, w_ref[:, :d]) + b_ref[:, :d]
    g = _mm(xb, w_ref[:, d:]) + b_ref[:, d:]
    u_ref[...] = a * jax.nn.sigmoid(g)


def glu(x, w_bf, b):
    m, d = x.shape
    tm = _row_tile(m)
    return pl.pallas_call(
        _glu_kernel,
        grid=(m // tm,),
        in_specs=[pl.BlockSpec((tm, d), lambda i: (i, 0)), _resident((d, 2 * d)), _resident((1, 2 * d))],
        out_specs=pl.BlockSpec((tm, d), lambda i: (i, 0)),
        out_shape=jax.ShapeDtypeStruct((m, d), F32),
        compiler_params=_params(1),
        name="glu",
    )(x, w_bf, b.reshape(1, 2 * d))


CONV_HALO = 32
CONV_ROWS = 64


def _conv_prompt_kernel(u_ref, halo_ref, wdw_ref, bdw_ref, g_ref, b_ref, z_ref, win_ref, y_ref):
    i = pl.program_id(0)
    tm = u_ref.shape[0]
    ncb = win_ref.shape[0]
    halo = jnp.where(i > 0, halo_ref[...], 0.0)
    for cb in range(ncb):
        win_ref[cb, 0:CONV_HALO, :] = halo[:, cb * LANES:(cb + 1) * LANES]
        win_ref[cb, CONV_HALO:CONV_HALO + tm, :] = u_ref[:, cb * LANES:(cb + 1) * LANES]
    shift = CONV_HALO - CONV_STATE

    def cb_body(cb, carry):
        w = wdw_ref[cb]
        for r in range(tm // CONV_ROWS):
            acc = jnp.zeros((CONV_ROWS, LANES), F32)
            for j in range(CONV_WIDTH):
                acc = acc + win_ref[cb, pl.ds(r * CONV_ROWS + shift + j, CONV_ROWS), :] * w[j:j + 1, :]
            y_ref[cb, r * CONV_ROWS:(r + 1) * CONV_ROWS, :] = acc
        return carry

    lax.fori_loop(0, ncb, cb_body, 0)
    y = jnp.concatenate([y_ref[cb] for cb in range(ncb)], axis=1) + bdw_ref[...]
    z = _layer_norm(y, g_ref[...], b_ref[...])
    z_ref[...] = (z * jax.nn.sigmoid(z)).astype(BF16)


def conv_prompt(u, w_dw, b_dw, g_n, b_n):
    m, d = u.shape
    tm = _row_tile(m)
    ncb = d // LANES
    wdw = jnp.pad(w_dw, ((0, 1), (0, 0))).reshape(CONV_WIDTH + 1, ncb, LANES).transpose(1, 0, 2)
    hb = tm // CONV_HALO
    return pl.pallas_call(
        _conv_prompt_kernel,
        grid=(m // tm,),
        in_specs=[pl.BlockSpec((tm, d), lambda i: (i, 0)),
                  pl.BlockSpec((CONV_HALO, d), lambda i: (jnp.maximum(i * hb - 1, 0), 0)),
                  _resident((ncb, CONV_WIDTH + 1, LANES)),
                  _resident((1, d)), _resident((1, d)), _resident((1, d))],
        out_specs=pl.BlockSpec((tm, d), lambda i: (i, 0)),
        out_shape=jax.ShapeDtypeStruct((m, d), BF16),
        scratch_shapes=[pltpu.VMEM((ncb, tm + CONV_HALO, LANES), F32), pltpu.VMEM((ncb, tm, LANES), F32)],
        compiler_params=_params(1),
        name="conv_prompt",
    )(u, u, wdw, b_dw.reshape(1, d), g_n.reshape(1, d), b_n.reshape(1, d))


def _conv_sample_kernel(st_ref, u_ref, wdw_ref, bdw_ref, g_ref, b_ref, z_ref, ns_ref, win_ref):
    t = u_ref.shape[0]
    win_ref[0:CONV_STATE, :] = st_ref[...]
    win_ref[CONV_STATE:CONV_STATE + t, :] = u_ref[...]
    acc = jnp.zeros(u_ref.shape, F32)
    for j in range(CONV_WIDTH):
        acc = acc + win_ref[j:j + t, :] * wdw_ref[j:j + 1, :]
    z = _layer_norm(acc + bdw_ref[...], g_ref[...], b_ref[...])
    z_ref[...] = z * jax.nn.sigmoid(z)
    ns_ref[...] = win_ref[t:t + CONV_STATE, :]


def conv_sample(u, state, w_dw, b_dw, g_n, b_n):
    b, t, d = u.shape
    return pl.pallas_call(
        _conv_sample_kernel,
        grid=(b,),
        in_specs=[pl.BlockSpec((None, CONV_STATE, d), lambda i: (i, 0, 0)),
                  pl.BlockSpec((None, t, d), lambda i: (i, 0, 0)),
                  _resident((CONV_WIDTH + 1, d)), _resident((1, d)), _resident((1, d)), _resident((1, d))],
        out_specs=[pl.BlockSpec((None, t, d), lambda i: (i, 0, 0)),
                   pl.BlockSpec((None, CONV_STATE, d), lambda i: (i, 0, 0))],
        out_shape=[jax.ShapeDtypeStruct((b, t, d), F32), jax.ShapeDtypeStruct((b, CONV_STATE, d), F32)],
        scratch_shapes=[pltpu.VMEM((CONV_STATE + t + 2, d), F32)],
        compiler_params=_params(1),
        name="conv_sample",
    )(state, u, jnp.pad(w_dw, ((0, 1), (0, 0))), b_dw.reshape(1, d), g_n.reshape(1, d), b_n.reshape(1, d))


def _proj_res_ln_kernel(z_ref, w_ref, bias_ref, x_ref, g_ref, b_ref, o_ref):
    m = _mm(z_ref[...].astype(BF16), w_ref[...]) + bias_ref[...]
    o_ref[...] = _layer_norm(ALPHA * x_ref[...] + m, g_ref[...], b_ref[...])


def proj_res_ln(z, w_bf, bias, x, g, b):
    m, d = x.shape
    k = z.shape[1]
    tm = _row_tile(m)
    return pl.pallas_call(
        _proj_res_ln_kernel,
        grid=(m // tm,),
        in_specs=[pl.BlockSpec((tm, k), lambda i: (i, 0)), _resident((k, d)), _resident((1, d)),
                  pl.BlockSpec((tm, d), lambda i: (i, 0)), _resident((1, d)), _resident((1, d))],
        out_specs=pl.BlockSpec((tm, d), lambda i: (i, 0)),
        out_shape=jax.ShapeDtypeStruct((m, d), F32),
        compiler_params=_params(1),
        name="proj_res_ln",
    )(z, w_bf, bias.reshape(1, d), x, g.reshape(1, d), b.reshape(1, d))


FFN_CHUNK = 256


def _ffn_kernel(x_ref, win_ref, wout_ref, g_ref, b_ref, o_ref):
    x = x_ref[...]
    xb = x.astype(BF16)
    dff = wout_ref.shape[0]
    acc = jnp.zeros(x.shape, F32)
    for c in range(dff // FFN_CHUNK):
        lo = c * FFN_CHUNK
        gate = _mm(xb, win_ref[:, lo:lo + FFN_CHUNK])
        up = _mm(xb, win_ref[:, dff + lo:dff + lo + FFN_CHUNK])
        act = (gate * jax.nn.sigmoid(gate) * up).astype(BF16)
        acc = acc + _mm(act, wout_ref[lo:lo + FFN_CHUNK, :])
    o_ref[...] = _layer_norm(ALPHA * x + acc, g_ref[...], b_ref[...])


def ffn(x, win_bf, wout_bf, g, b):
    m, d = x.shape
    dff = wout_bf.shape[0]
    tm = _row_tile(m)
    return pl.pallas_call(
        _ffn_kernel,
        grid=(m // tm,),
        in_specs=[pl.BlockSpec((tm, d), lambda i: (i, 0)), _resident((d, 2 * dff)), _resident((dff, d)),
                  _resident((1, d)), _resident((1, d))],
        out_specs=pl.BlockSpec((tm, d), lambda i: (i, 0)),
        out_shape=jax.ShapeDtypeStruct((m, d), F32),
        compiler_params=_params(1),
        name="ffn",
    )(x, win_bf, wout_bf, g.reshape(1, d), b.reshape(1, d))


Q_SCALE = HEAD_DIM ** -0.5 * LOG2E
W_SCALE = IDX_HEADS ** -0.5 * IDX_DIM ** -0.5
QI_PAD = IDX_HEADS * LANES


def _split_attn_weights(w_in):
    d = D_MODEL
    o3 = 3 * d
    o4 = o3 + IDX_HEADS * IDX_DIM
    o5 = o4 + IDX_DIM
    wq, wk, wv = w_in[:, :d], w_in[:, d:2 * d], w_in[:, 2 * d:o3]
    wqi = w_in[:, o3:o4].reshape(d, IDX_HEADS, IDX_DIM)
    wqi = jnp.pad(wqi, ((0, 0), (0, 0), (0, LANES - IDX_DIM))).reshape(d, QI_PAD)
    wkw = jnp.pad(w_in[:, o4:], ((0, 0), (0, LANES - IDX_DIM - IDX_HEADS)))
    return tuple(a.astype(BF16) for a in (wq, wk, wv, wqi, wkw))


def _attn_proj_t_kernel(x_ref, wk_ref, wv_ref, wkw_ref, wqT_ref, wvT_ref, wqiT_ref, wwT_ref,
                        k32_ref, v32_ref, kiw_ref, qT_ref, kbf_ref, vT_ref, qiT_ref, kip_ref, wT_ref):
    xb = x_ref[...].astype(BF16)
    k = _mm(xb, wk_ref[...])
    k32_ref[...] = k
    kbf_ref[...] = k.astype(BF16)
    v32_ref[...] = _mm(xb, wv_ref[...])
    kiw = _mm(xb, wkw_ref[...])
    kiw_ref[...] = kiw
    lane = lax.broadcasted_iota(I32, kiw.shape, 1)
    kip_ref[...] = jnp.where(lane < IDX_DIM, kiw, 0.0).astype(BF16)
    qT_ref[...] = (_mm_nt(wqT_ref[...], xb) * Q_SCALE).astype(BF16)
    vT_ref[...] = _mm_nt(wvT_ref[...], xb).astype(BF16)
    qiT_ref[...] = _mm_nt(wqiT_ref[...], xb).astype(BF16)
    wT_ref[...] = _mm_nt(wwT_ref[...], xb) * W_SCALE


def attn_proj_t(x, w_in):
    m, d = x.shape
    tm = _row_tile(m)
    wq, wk, wv, wqi, wkw = _split_attn_weights(w_in)
    o5 = 3 * d + IDX_HEADS * IDX_DIM + IDX_DIM
    wwT = jnp.pad(w_in[:, o5:].T, ((0, 16 - IDX_HEADS), (0, 0))).astype(BF16)
    row = lambda n: pl.BlockSpec((tm, n), lambda i: (i, 0))
    col = lambda n: pl.BlockSpec((n, tm), lambda i: (0, i))
    return pl.pallas_call(
        _attn_proj_t_kernel,
        grid=(m // tm,),
        in_specs=[row(d), _resident((d, d)), _resident((d, d)), _resident((d, LANES)),
                  _resident((d, d)), _resident((d, d)), _resident((QI_PAD, d)), _resident((16, d))],
        out_specs=[row(d), row(d), row(LANES), col(d), row(d), col(d), col(QI_PAD), row(LANES), col(16)],
        out_shape=[jax.ShapeDtypeStruct((m, d), F32), jax.ShapeDtypeStruct((m, d), F32),
                   jax.ShapeDtypeStruct((m, LANES), F32), jax.ShapeDtypeStruct((d, m), BF16),
                   jax.ShapeDtypeStruct((m, d), BF16), jax.ShapeDtypeStruct((d, m), BF16),
                   jax.ShapeDtypeStruct((QI_PAD, m), BF16), jax.ShapeDtypeStruct((m, LANES), BF16),
                   jax.ShapeDtypeStruct((16, m), F32)],
        compiler_params=_params(1),
        name="attn_proj_t",
    )(x, wk, wv, wkw, wq.T, wv.T, wqi.T, wwT)


def _attn_proj_s_kernel(x_ref, wq_ref, wk_ref, wv_ref, wqi_ref, wkw_ref,
                        q_ref, k32_ref, v32_ref, qi_ref, kiw_ref):
    xb = x_ref[...].astype(BF16)
    q_ref[...] = (_mm(xb, wq_ref[...]) * Q_SCALE).astype(BF16)
    k32_ref[...] = _mm(xb, wk_ref[...])
    v32_ref[...] = _mm(xb, wv_ref[...])
    qi_ref[...] = _mm(xb, wqi_ref[...]).astype(BF16)
    kiw_ref[...] = _mm(xb, wkw_ref[...])


def attn_proj_s(x, w_in):
    m, d = x.shape
    tm = _row_tile(m)
    wq, wk, wv, wqi, wkw = _split_attn_weights(w_in)
    row = lambda n: pl.BlockSpec((tm, n), lambda i: (i, 0))
    return pl.pallas_call(
        _attn_proj_s_kernel,
        grid=(m // tm,),
        in_specs=[row(d), _resident((d, d)), _resident((d, d)), _resident((d, d)),
                  _resident((d, QI_PAD)), _resident((d, LANES))],
        out_specs=[row(d), row(d), row(d), row(QI_PAD), row(LANES)],
        out_shape=[jax.ShapeDtypeStruct((m, d), BF16), jax.ShapeDtypeStruct((m, d), F32),
                   jax.ShapeDtypeStruct((m, d), F32), jax.ShapeDtypeStruct((m, QI_PAD), BF16),
                   jax.ShapeDtypeStruct((m, LANES), F32)],
        compiler_params=_params(1),
        name="attn_proj_s",
    )(x, wq, wk, wv, wqi, wkw)


def _prompt_schedule(t):
    qb_of, kt_of, fl = [], [], []
    for qb in range(t // BQ):
        nb = -(-max(qb * BQ - CH, 0) // TK)
        qb_of.append(qb); kt_of.append(0); fl.append(1 | (2 if nb == 0 else 0))
        for kt in range(nb):
            qb_of.append(qb); kt_of.append(kt); fl.append(2 if kt == nb - 1 else 0)
    return (np.asarray(qb_of, np.int32), np.asarray(kt_of, np.int32), np.asarray(fl, np.int32))


def _attn_prompt_kernel(qb_ref, kt_ref, fl_ref,
                        qT_ref, qiT_ref, wT_ref, kip_ref, kb_ref, vTb_ref, kp_ref, kd_ref, vTp_ref, vTd_ref,
                        rb_ref, o_ref,
                        s_ref, thr_ref, bb_ref, m_ref, l_ref, acc_ref, madd_ref, *, k_sel, idx_bits):
    step = pl.program_id(0)
    qb = qb_ref[step]
    kt = kt_ref[step]
    flags = fl_ref[step]
    is_band = (flags & 1) != 0
    is_last = (flags & 2) != 0
    q0 = qb * BQ
    nch = 2 * (qb + 1)
    cband = 2 * qb - 1
    kf = float(k_sel)

    @pl.when(step == 0)
    def _build_band_bias():
        r = lax.broadcasted_iota(I32, (BAND, BQ), 0)
        j = lax.broadcasted_iota(I32, (BAND, BQ), 1)
        dist = CH + j - r
        for h in range(N_HEADS):
            bb_ref[h] = _bias_from_dist(dist, rb_ref, h)

    def head_cols(h):
        return slice(h * HEAD_DIM, (h + 1) * HEAD_DIM)

    def qk(h, kh):
        n = kh.shape[0] // QK_ROWS
        return jnp.concatenate([_mm(kh[i * QK_ROWS:(i + 1) * QK_ROWS], qT_ref[head_cols(h), :])
                                for i in range(n)], axis=0)

    def softmax_pv(h, lt, vth, madd, bias, cbias):
        hs = head_cols(h)
        if bias is not None:
            lt = lt + bias
        lt = lt + madd
        mx = jnp.max(lt, axis=0, keepdims=True) + cbias
        m_old = m_ref[h:h + 1, :]
        m_new = jnp.maximum(m_old, mx)
        alpha = jnp.exp2(m_old - m_new)
        p = jnp.exp2(lt - (m_new - cbias))
        psum = p[0:SUBLANES]
        for i in range(1, p.shape[0] // SUBLANES):
            psum = psum + p[i * SUBLANES:(i + 1) * SUBLANES]
        l_ref[h] = alpha * l_ref[h] + psum
        pb = p.astype(BF16)
        r = pb.shape[0]
        if r % (2 * PV_KEYS) == 0:
            pv = _mm(vth[:, :r // 2], pb[:r // 2]) + _mm(vth[:, r // 2:], pb[r // 2:])
        else:
            pv = _mm(vth, pb)
        acc_ref[hs, :] = alpha * acc_ref[hs, :] + pv
        m_ref[h:h + 1, :] = m_new

    def attend_heads(keys_of, vals_of, madd, bias_of, cbias_of):
        lt = qk(0, keys_of(0))
        for h in range(N_HEADS):
            lt_next = qk(h + 1, keys_of(h + 1)) if h + 1 < N_HEADS else None
            softmax_pv(h, lt, vals_of(h), madd, bias_of(h), cbias_of(h))
            lt = lt_next

    @pl.when(is_band)
    def _band_step():
        def idx_body(gi, carry):
            rows = IDX_GROUP * CH
            kic = kip_ref[pl.ds(pl.multiple_of(gi * rows, rows), rows), :]
            acc = None
            for h in range(IDX_HEADS):
                sc = _mm(kic, qiT_ref[h * LANES:(h + 1) * LANES, :])
                val = jnp.maximum(sc, 0.0) * wT_ref[h:h + 1, :]
                acc = val if acc is None else acc + val
            for i in range(IDX_GROUP):
                s_ref[gi * IDX_GROUP + i] = acc[i * CH:(i + 1) * CH]
            return carry

        ngrp = (nch + IDX_GROUP - 1) // IDX_GROUP
        lax.fori_loop(0, ngrp, idx_body, 0)
        rr = lax.broadcasted_iota(I32, (CH, BQ), 0)
        jj = lax.broadcasted_iota(I32, (CH, BQ), 1)
        for i in range(BQ // CH):
            c = 2 * qb + i
            s_ref[c] = jnp.where(i * CH + rr <= jj, s_ref[c], NEG_INF)

        @pl.when(ngrp * IDX_GROUP > nch)
        def _mask_group_tail():
            for c in range(IDX_GROUP - BQ // CH):
                s_ref[nch + c] = jnp.full((CH, BQ), NEG_INF, F32)

        def count(pred):
            def body(g, acc):
                for c in range(IDX_GROUP):
                    ci = g * IDX_GROUP + c
                    ind = jnp.where(pred(ci, s_ref[ci]), 1.0, 0.0)
                    for r in range(CH // SUBLANES):
                        acc = acc + ind[r * SUBLANES:(r + 1) * SUBLANES]
                return acc
            acc = lax.fori_loop(0, ngrp, body, jnp.zeros((SUBLANES, BQ), F32))
            return jnp.broadcast_to(jnp.sum(acc, axis=0, keepdims=True), (SUBLANES, BQ))

        def count_ge(cand_key):
            cand = _key_to_f32(cand_key)[0:1, :]
            return count(lambda c, x: x >= cand)

        total = (ngrp * (IDX_GROUP * CH)).astype(F32)
        c0 = count(lambda c, x: x >= 0.0)
        ans = jnp.where(c0 >= kf, 0, INT_MIN).astype(I32)
        cnt = jnp.where(c0 >= kf, c0, total)

        def bit_cond(carry):
            i, _, cnt = carry
            return jnp.logical_and(i < 31, jnp.max(jnp.abs(cnt - kf)) > 0.5)

        def bit_body(carry):
            i, ans, cnt = carry
            cand = ans + jnp.left_shift(jnp.int32(1), 30 - i)
            c = count_ge(cand)
            ok = jnp.logical_and(c >= kf, cnt != kf)
            return i + 1, jnp.where(ok, cand, ans), jnp.where(ok, c, cnt)

        _, ans, cnt = lax.while_loop(bit_cond, bit_body, (jnp.int32(0), ans, cnt))
        none = ans == INT_MIN
        thr = jnp.where(none, NEG_INF, _key_to_f32(ans))
        thr_ref[...] = thr

        has = jnp.logical_and(cnt > kf, jnp.logical_not(none))
        any_ties = jnp.max(jnp.where(has, 1.0, 0.0)) > 0.5

        @pl.when(any_ties)
        def _resolve_ties():
            t1 = thr[0:1, :]
            n_gt = count_ge(ans + 1)
            need = jnp.where(has, kf - n_gt, 3.0e38)
            rows = lax.broadcasted_iota(I32, (CH, BQ), 0)

            def j_body(i, j):
                cj = j + jnp.left_shift(jnp.int32(1), idx_bits - 1 - i)
                c1 = cj[0:1, :]
                f = count(lambda c, x: jnp.logical_and(x == t1, c * CH + rows < c1))
                return jnp.where(f < need, cj, j)

            jstar = lax.fori_loop(0, idx_bits, j_body, jnp.zeros((SUBLANES, BQ), I32))
            j1 = jstar[0:1, :]
            h1 = has[0:1, :]

            def demote(c, carry):
                x = s_ref[c]
                drop = jnp.logical_and(jnp.logical_and(x == t1, c * CH + rows > j1), h1)
                s_ref[c] = jnp.where(drop, NEG_INF, x)
                return carry

            lax.fori_loop(0, nch, demote, 0)

        m_ref[...] = jnp.full(m_ref.shape, M_FLOOR, F32)
        l_ref[...] = jnp.zeros(l_ref.shape, F32)
        acc_ref[...] = jnp.zeros(acc_ref.shape, F32)
        thr1 = thr_ref[0:1, :]
        cprev = jnp.maximum(cband, 0)
        sc = jnp.concatenate([s_ref[cprev], s_ref[2 * qb], s_ref[2 * qb + 1]], axis=0)
        r = lax.broadcasted_iota(I32, (BAND, BQ), 0)
        j = lax.broadcasted_iota(I32, (BAND, BQ), 1)
        ok = jnp.logical_and(sc >= thr1, r - CH <= j)
        ok = jnp.logical_and(ok, jnp.logical_or(r >= CH, qb > 0))
        madd = jnp.where(ok, 0.0, NEG_INF)
        attend_heads(lambda h: jnp.concatenate([kp_ref[:, head_cols(h)], kd_ref[:, head_cols(h)]], axis=0),
                     lambda h: jnp.concatenate([vTp_ref[head_cols(h), :], vTd_ref[head_cols(h), :]], axis=1),
                     madd, lambda h: bb_ref[h], lambda h: 0.0)

    @pl.when(jnp.logical_not(is_band))
    def _bulk_step():
        thr1 = thr_ref[0:1, :]
        for i in range(TK // CH):
            c = kt * (TK // CH) + i
            x = s_ref[jnp.minimum(c, cband - 1)]
            sel = jnp.where(x >= thr1, 0.0, NEG_INF)
            madd_ref[i * CH:(i + 1) * CH, :] = jnp.where(c < cband, sel, NEG_INF)
        attend_heads(lambda h: kb_ref[:, head_cols(h)], lambda h: vTb_ref[head_cols(h), :], madd_ref[...],
                     lambda h: None, lambda h: rb_ref[REL_BUCKETS - 1, h] * LOG2E)

    @pl.when(is_last)
    def _finalize():
        for h in range(N_HEADS):
            hs = slice(h * HEAD_DIM, (h + 1) * HEAD_DIM)
            lsum = jnp.sum(l_ref[h], axis=0, keepdims=True)
            o_ref[:, hs] = (acc_ref[hs, :] / lsum).T.astype(BF16)


def attn_prompt(qT, qiT, wT, kip, kbf, vT, rel_bias):
    d, t = qT.shape
    k_sel = min(TOPK_MAX, t // 4)
    qb_of, kt_of, fl = _prompt_schedule(t)
    n_steps = len(qb_of)
    tk = min(TK, t)
    qcol = lambda n: pl.BlockSpec((n, BQ), lambda s, qb, kt, fl: (0, qb[s]))
    in_specs = [
        qcol(d), qcol(QI_PAD), qcol(16),
        pl.BlockSpec((t, LANES), lambda s, qb, kt, fl: (0, 0), pipeline_mode=pl.Buffered(1)),
        pl.BlockSpec((tk, d), lambda s, qb, kt, fl: (kt[s], 0)),
        pl.BlockSpec((d, tk), lambda s, qb, kt, fl: (0, kt[s])),
        pl.BlockSpec((CH, d), lambda s, qb, kt, fl: (jnp.maximum(2 * qb[s] - 1, 0), 0)),
        pl.BlockSpec((BQ, d), lambda s, qb, kt, fl: (qb[s], 0)),
        pl.BlockSpec((d, CH), lambda s, qb, kt, fl: (0, jnp.maximum(2 * qb[s] - 1, 0))),
        pl.BlockSpec((d, BQ), lambda s, qb, kt, fl: (0, qb[s])),
        pl.BlockSpec(memory_space=pltpu.SMEM),
    ]
    grid_spec = pltpu.PrefetchScalarGridSpec(
        num_scalar_prefetch=3,
        grid=(n_steps,),
        in_specs=in_specs,
        out_specs=pl.BlockSpec((BQ, d), lambda s, qb, kt, fl: (qb[s], 0)),
        scratch_shapes=[
            pltpu.VMEM((t // CH, CH, BQ), F32),
            pltpu.VMEM((SUBLANES, BQ), F32),
            pltpu.VMEM((N_HEADS, BAND, BQ), F32),
            pltpu.VMEM((N_HEADS, BQ), F32),
            pltpu.VMEM((N_HEADS, SUBLANES, BQ), F32),
            pltpu.VMEM((d, BQ), F32),
            pltpu.VMEM((tk, BQ), F32),
        ],
    )
    kern = functools.partial(_attn_prompt_kernel, k_sel=k_sel, idx_bits=max(t.bit_length() - 1, 1))
    return pl.pallas_call(
        kern,
        grid_spec=grid_spec,
        out_shape=jax.ShapeDtypeStruct((t, d), BF16),
        compiler_params=_params(1),
        name="attn_prompt",
    )(jnp.asarray(qb_of), jnp.asarray(kt_of), jnp.asarray(fl),
      qT, qiT, wT, kip, kbf, vT, kbf, kbf, vT, vT, rel_bias)


def _sample_scores_kernel(pt_ref, qi_ref, w_ref, *rest, n_groups):
    pages = rest[:S1_PAGES]
    kin_ref, o_ref, kbuf_ref = rest[S1_PAGES:]
    b = pl.program_id(0)
    g = pl.program_id(1)
    t = o_ref.shape[1]

    @pl.when(jnp.logical_and(b == 0, g == 0))
    def _zero_pad_rows():
        kbuf_ref[...] = jnp.zeros(kbuf_ref.shape, BF16)

    def head_sum(val):
        s = val[0:t]
        for h in range(1, IDX_HEADS):
            s = s + val[h * t:(h + 1) * t]
        return s

    wcol = w_ref[:, 0:1]

    @pl.when(g < n_groups)
    def _past_pages():
        for i in range(S1_PAGES):
            kbuf_ref[0:IDX_DIM, i * PAGE_SIZE:(i + 1) * PAGE_SIZE] = pages[i][...].astype(BF16)
        s = head_sum(jnp.maximum(_mm(qi_ref[...], kbuf_ref[...]), 0.0) * wcol)
        for i in range(S1_PAGES):
            o_ref[i] = s[:, i * PAGE_SIZE:(i + 1) * PAGE_SIZE]

    @pl.when(g == n_groups)
    def _new_keys():
        s = head_sum(jnp.maximum(_mm_nt(qi_ref[...], kin_ref[...]), 0.0) * wcol)
        row = lax.broadcasted_iota(I32, s.shape, 0)
        lane = lax.broadcasted_iota(I32, s.shape, 1)
        o_ref[0] = jnp.where(lane <= row, s, NEG_INF)
        for i in range(1, S1_PAGES):
            o_ref[i] = jnp.full(s.shape, NEG_INF, F32)


def sample_scores(page_table, qi_ht, w_ht, cache_kidx, layer, ki_new):
    b, n_pages = page_table.shape
    t = qi_ht.shape[1] // IDX_HEADS
    n_groups = n_pages // S1_PAGES
    page_spec = lambda i: pl.BlockSpec(
        (None, None, IDX_DIM, PAGE_SIZE),
        lambda bi, g, pt: (layer, pt[bi, jnp.minimum(g, n_groups - 1) * S1_PAGES + i], 0, 0))
    grid_spec = pltpu.PrefetchScalarGridSpec(
        num_scalar_prefetch=1,
        grid=(b, n_groups + 1),
        in_specs=[pl.BlockSpec((None, IDX_HEADS * t, LANES), lambda bi, g, pt: (bi, 0, 0)),
                  pl.BlockSpec((None, IDX_HEADS * t, LANES), lambda bi, g, pt: (bi, 0, 0))]
                 + [page_spec(i) for i in range(S1_PAGES)]
                 + [pl.BlockSpec((None, PAGE_SIZE, LANES), lambda bi, g, pt: (bi, 0, 0))],
        out_specs=pl.BlockSpec((None, S1_PAGES, t, LANES), lambda bi, g, pt: (bi, g, 0, 0)),
        scratch_shapes=[pltpu.VMEM((LANES, S1_PAGES * PAGE_SIZE), BF16)],
    )
    kidx_t = jnp.swapaxes(cache_kidx, 2, 3)
    return pl.pallas_call(
        functools.partial(_sample_scores_kernel, n_groups=n_groups),
        grid_spec=grid_spec,
        out_shape=jax.ShapeDtypeStruct((b, (n_groups + 1) * S1_PAGES, t, LANES), F32),
        compiler_params=_params(2),
        name="sample_scores",
    )(page_table, qi_ht, w_ht, *([kidx_t] * S1_PAGES), ki_new)


def _sample_select_kernel(s_ref, o_ref, *, k_sel, past, idx_bits):
    x = s_ref[...]
    nck, t, _ = x.shape
    kf = float(k_sel)
    cidx = lax.broadcasted_iota(I32, x.shape, 0)
    row = lax.broadcasted_iota(I32, x.shape, 1)
    lane = lax.broadcasted_iota(I32, x.shape, 2)
    idx = cidx * LANES + lane

    def count(pred):
        per_lane = jnp.sum(jnp.where(pred, 1.0, 0.0), axis=0)
        return jnp.broadcast_to(jnp.sum(per_lane, axis=1, keepdims=True), (t, LANES))

    def count_ge(cand_key):
        return count(x >= _key_to_f32(cand_key)[None])

    total = float(nck * LANES)
    c0 = count_ge(jnp.zeros((t, LANES), I32))
    ans = jnp.where(c0 >= kf, 0, INT_MIN).astype(I32)
    cnt = jnp.where(c0 >= kf, c0, total)

    def bit_cond(carry):
        i, _, cnt = carry
        return jnp.logical_and(i < 31, jnp.max(jnp.abs(cnt - kf)) > 0.5)

    def bit_body(carry):
        i, ans, cnt = carry
        cand = ans + jnp.left_shift(jnp.int32(1), 30 - i)
        c = count_ge(cand)
        ok = jnp.logical_and(c >= kf, cnt != kf)
        return i + 1, jnp.where(ok, cand, ans), jnp.where(ok, c, cnt)

    _, ans, cnt = lax.while_loop(bit_cond, bit_body, (jnp.int32(0), ans, cnt))
    none = ans == INT_MIN
    thr = jnp.where(none, NEG_INF, _key_to_f32(ans))
    has = jnp.logical_and(cnt > kf, jnp.logical_not(none))
    any_ties = jnp.max(jnp.where(has, 1.0, 0.0)) > 0.5
    new = idx - past
    valid = jnp.logical_or(idx < past, jnp.logical_and(new < t, new <= row))
    keep = jnp.logical_and(x >= thr[None], valid)

    @pl.when(jnp.logical_not(any_ties))
    def _no_ties():
        o_ref[...] = jnp.where(keep, 0.0, NEG_INF)

    @pl.when(any_ties)
    def _resolve_ties():
        n_gt = count_ge(ans + 1)
        need = jnp.where(has, kf - n_gt, 3.0e38)
        eq = x == thr[None]

        def j_body(i, j):
            cj = j + jnp.left_shift(jnp.int32(1), idx_bits - 1 - i)
            f = count(jnp.logical_and(eq, idx < cj[None]))
            return jnp.where(f < need, cj, j)

        jstar = lax.fori_loop(0, idx_bits, j_body, jnp.zeros((t, LANES), I32))
        drop = jnp.logical_and(jnp.logical_and(eq, idx > jstar[None]), has[None])
        o_ref[...] = jnp.where(jnp.logical_and(keep, jnp.logical_not(drop)), 0.0, NEG_INF)


def sample_select(scores, past, k_sel):
    b, nck, t, _ = scores.shape
    spec = pl.BlockSpec((None, nck, t, LANES), lambda i: (i, 0, 0, 0))
    kern = functools.partial(_sample_select_kernel, k_sel=k_sel, past=past,
                             idx_bits=(nck * LANES - 1).bit_length())
    return pl.pallas_call(
        kern, grid=(b,), in_specs=[spec], out_specs=spec,
        out_shape=jax.ShapeDtypeStruct(scores.shape, F32),
        compiler_params=_params(1),
        name="sample_select",
    )(scores)


def _sample_attn_kernel(pt_ref, q_ref, madd_ref, *rest, n_groups, t):
    kpages = rest[:S3_PAGES]
    vpages = rest[S3_PAGES:2 * S3_PAGES]
    (kn_ref, vn_ref, rb_ref, o_ref,
     kcat_ref, vcat_ref, m_ref, l_ref, acc_ref, bconst_ref, blast_ref, bnew_ref) = rest[2 * S3_PAGES:]
    b = pl.program_id(0)
    g = pl.program_id(1)
    rows = N_HEADS * t

    @pl.when(jnp.logical_and(b == 0, g == 0))
    def _build_bias_tables():
        tq = lax.broadcasted_iota(I32, (t, LANES), 0)
        ln = lax.broadcasted_iota(I32, (t, LANES), 1)
        for h in range(N_HEADS):
            hs = slice(h * t, (h + 1) * t)
            bconst_ref[hs, :] = jnp.full((t, LANES), rb_ref[REL_BUCKETS - 1, h] * LOG2E, F32)
            blast_ref[hs, :] = _bias_from_dist(PAGE_SIZE + tq - ln, rb_ref, h)
            bnew_ref[hs, :] = _bias_from_dist(tq - ln, rb_ref, h)

    @pl.when(g == 0)
    def _init():
        m_ref[...] = jnp.full(m_ref.shape, M_FLOOR, F32)
        l_ref[...] = jnp.zeros(l_ref.shape, F32)
        acc_ref[...] = jnp.zeros(acc_ref.shape, F32)

    def tile_heads(m8):
        return jnp.concatenate([m8] * N_HEADS, axis=0)

    def attend(kk, vv, bias, madd):
        lg = _mm_nt(q_ref[...], kk) + bias + madd
        mx = jnp.max(lg, axis=1, keepdims=True)
        m_old = m_ref[...]
        m_new = jnp.maximum(m_old, mx)
        alpha = jnp.exp2(m_old - m_new)
        p = jnp.exp2(lg - m_new[:, 0:1])
        l_ref[...] = alpha * l_ref[...] + jnp.sum(p, axis=1, keepdims=True)
        acc_ref[...] = alpha[:, 0:1] * acc_ref[...] + _mm(p.astype(BF16), vv)
        m_ref[...] = m_new

    @pl.when(g < n_groups)
    def _cached_pages():
        for i in range(S3_PAGES):
            rs = slice(i * PAGE_SIZE, (i + 1) * PAGE_SIZE)
            for h in range(N_HEADS):
                hs = slice(h * HEAD_DIM, (h + 1) * HEAD_DIM)
                head_rows = pl.ds(h, PAGE_SIZE, stride=N_HEADS)
                kcat_ref[rs, hs] = kpages[i][head_rows, :].astype(BF16)
                vcat_ref[rs, hs] = vpages[i][head_rows, :].astype(BF16)
        last = jnp.where(g == n_groups - 1, blast_ref[...], bconst_ref[...])
        bias = jnp.concatenate([bconst_ref[...]] * (S3_PAGES - 1) + [last], axis=1)
        madd = jnp.concatenate([tile_heads(madd_ref[i]) for i in range(S3_PAGES)], axis=1)
        attend(kcat_ref[...], vcat_ref[...], bias, madd)

    @pl.when(g == n_groups)
    def _new_keys_and_finish():
        attend(kn_ref[...], vn_ref[...], bnew_ref[...], tile_heads(madd_ref[0]))
        for h in range(N_HEADS):
            hs = slice(h * HEAD_DIM, (h + 1) * HEAD_DIM)
            rs = slice(h * t, (h + 1) * t)
            o_ref[:, hs] = acc_ref[rs, hs] / l_ref[rs, 0:1]


def sample_attn(page_table, q_bd, madd, cache_k, cache_v, layer, k_new, v_new, rel_bias):
    b, n_pages = page_table.shape
    rows, d = q_bd.shape[1:]
    t = rows // N_HEADS
    n_groups = n_pages // S3_PAGES
    page_spec = lambda i: pl.BlockSpec(
        (None, None, PAGE_SIZE * N_HEADS, HEAD_DIM),
        lambda bi, g, pt: (layer, pt[bi, jnp.minimum(g, n_groups - 1) * S3_PAGES + i], 0, 0))
    per_b = lambda shape: pl.BlockSpec((None,) + shape, lambda bi, g, pt: (bi,) + (0,) * len(shape))
    grid_spec = pltpu.PrefetchScalarGridSpec(
        num_scalar_prefetch=1,
        grid=(b, n_groups + 1),
        in_specs=[per_b((rows, d)),
                  pl.BlockSpec((None, S3_PAGES, t, LANES), lambda bi, g, pt: (bi, g, 0, 0))]
                 + [page_spec(i) for i in range(S3_PAGES)] * 2
                 + [per_b((PAGE_SIZE, d)), per_b((PAGE_SIZE, d)), pl.BlockSpec(memory_space=pltpu.SMEM)],
        out_specs=per_b((t, d)),
        scratch_shapes=[pltpu.VMEM((S3_PAGES * PAGE_SIZE, d), BF16), pltpu.VMEM((S3_PAGES * PAGE_SIZE, d), BF16),
                        pltpu.VMEM((rows, LANES), F32), pltpu.VMEM((rows, LANES), F32),
                        pltpu.VMEM((rows, d), F32),
                        pltpu.VMEM((rows, LANES), F32), pltpu.VMEM((rows, LANES), F32),
                        pltpu.VMEM((rows, LANES), F32)],
    )
    return pl.pallas_call(
        functools.partial(_sample_attn_kernel, n_groups=n_groups, t=t),
        grid_spec=grid_spec,
        out_shape=jax.ShapeDtypeStruct((b, t, d), F32),
        compiler_params=_params(2),
        name="sample_attn",
    )(page_table, q_bd, madd, *([cache_k] * S3_PAGES), *([cache_v] * S3_PAGES), k_new, v_new, rel_bias)


def attn_sample(x, cache_k, cache_v, cache_kidx, page_table, layer, w_in, rel_bias):
    b, t, d = x.shape
    past = page_table.shape[1] * PAGE_SIZE
    k_sel = min(TOPK_MAX, (past + t) // 4)
    q, k32, v32, qi, kiw = attn_proj_s(x.reshape(b * t, d), w_in)
    qi_ht = qi.reshape(b, t, IDX_HEADS, LANES).transpose(0, 2, 1, 3).reshape(b, IDX_HEADS * t, LANES)
    wi = kiw[:, IDX_DIM:IDX_DIM + IDX_HEADS] * W_SCALE
    w_ht = jnp.broadcast_to(wi.reshape(b, t, IDX_HEADS).transpose(0, 2, 1).reshape(b, IDX_HEADS * t, 1),
                            (b, IDX_HEADS * t, LANES))
    head_of_col = jnp.arange(d, dtype=I32) // HEAD_DIM
    q_bd = jnp.where(head_of_col[None, None, None, :] == jnp.arange(N_HEADS, dtype=I32)[None, :, None, None],
                     q.reshape(b, 1, t, d), jnp.zeros((), BF16)).reshape(b, N_HEADS * t, d)
    pad_rows = lambda a: jnp.pad(a.reshape(b, t, -1), ((0, 0), (0, PAGE_SIZE - t), (0, 0))).astype(BF16)
    ki_new = pad_rows(jnp.pad(kiw[:, :IDX_DIM], ((0, 0), (0, LANES - IDX_DIM))))
    scores = sample_scores(page_table, qi_ht, w_ht, cache_kidx, layer, ki_new)
    madd = sample_select(scores, past, k_sel)
    page_rows = lambda c: c.reshape(c.shape[0], c.shape[1], PAGE_SIZE * N_HEADS, HEAD_DIM)
    out = sample_attn(page_table, q_bd, madd, page_rows(cache_k), page_rows(cache_v), layer,
                      pad_rows(k32), pad_rows(v32), rel_bias)
    return out.reshape(b * t, d), k32, v32, kiw[:, :IDX_DIM]


def kernel(x_prompt, x_sample, state_conv, cache_k, cache_v, cache_kidx, page_table, rel_bias,
           w_pw1, b_pw1, w_dw, b_dw, conv_norm_g, conv_norm_b, w_pw2, b_pw2,
           w_attn_in, w_attn_out, w_ffn_in, w_ffn_out, ln_mix_g, ln_mix_b, ln_ffn_g, ln_ffn_b):
    bp, tp, d = x_prompt.shape
    bs, ts, _ = x_sample.shape
    assert bp == 1 and d == D_MODEL and tp % TK == 0
    xp = x_prompt.reshape(tp, d)
    xs = x_sample.reshape(bs * ts, d)
    bf = lambda a: a.astype(BF16)

    w1, w2 = bf(w_pw1[0]), bf(w_pw2[0])
    up = glu(xp, w1, b_pw1[0])
    us = glu(xs, w1, b_pw1[0])
    zp = conv_prompt(up, w_dw[0], b_dw[0], conv_norm_g[0], conv_norm_b[0])
    zs, conv_s = conv_sample(us.reshape(bs, ts, d), state_conv[0], w_dw[0], b_dw[0],
                             conv_norm_g[0], conv_norm_b[0])
    conv_p = up[tp - CONV_STATE:]
    xp = proj_res_ln(zp, w2, b_pw2[0], xp, ln_mix_g[0], ln_mix_b[0])
    xs = proj_res_ln(zs.reshape(bs * ts, d), w2, b_pw2[0], xs, ln_mix_g[0], ln_mix_b[0])
    wf_in, wf_out = bf(w_ffn_in[0]), bf(w_ffn_out[0])
    xp = ffn(xp, wf_in, wf_out, ln_ffn_g[0], ln_ffn_b[0])
    xs = ffn(xs, wf_in, wf_out, ln_ffn_g[0], ln_ffn_b[0])

    w_o = bf(w_attn_out[0])
    no_bias = jnp.zeros((d,), F32)
    kp32, vp32, kiwp, qT, kbf, vT, qiT, kip, wT = attn_proj_t(xp, w_attn_in[0])
    ap = attn_prompt(qT, qiT, wT, kip, kbf, vT, rel_bias)
    xp = proj_res_ln(ap, w_o, no_bias, xp, ln_mix_g[1], ln_mix_b[1])
    a_s, ks32, vs32, kis = attn_sample(xs.reshape(bs, ts, d), cache_k, cache_v, cache_kidx, page_table, 0,
                                       w_attn_in[0], rel_bias)
    xs = proj_res_ln(a_s, w_o, no_bias, xs, ln_mix_g[1], ln_mix_b[1])
    wf_in, wf_out = bf(w_ffn_in[1]), bf(w_ffn_out[1])
    xp = ffn(xp, wf_in, wf_out, ln_ffn_g[1], ln_ffn_b[1])
    xs = ffn(xs, wf_in, wf_out, ln_ffn_g[1], ln_ffn_b[1])

    return (xp.reshape(1, tp, d), xs.reshape(bs, ts, d),
            conv_p.reshape(1, 1, CONV_STATE, d), conv_s.reshape(1, bs, CONV_STATE, d),
            kp32.reshape(1, 1, tp, N_HEADS, HEAD_DIM), vp32.reshape(1, 1, tp, N_HEADS, HEAD_DIM),
            kiwp[:, :IDX_DIM].reshape(1, 1, tp, IDX_DIM),
            ks32.reshape(1, bs, ts, N_HEADS, HEAD_DIM), vs32.reshape(1, bs, ts, N_HEADS, HEAD_DIM),
            kis.reshape(1, bs, ts, IDX_DIM))
```

```python
import functools
import math

import numpy as np
import jax
import jax.numpy as jnp
from jax import lax
from jax.experimental import pallas as pl
from jax.experimental.pallas import tpu as pltpu

F32, BF16, I32 = jnp.float32, jnp.bfloat16, jnp.int32

D_MODEL = 1024
N_HEADS = 8
HEAD_DIM = D_MODEL // N_HEADS
IDX_HEADS = 8
IDX_DIM = 64
TOPK_MAX = 256
CONV_WIDTH = 31
CONV_STATE = CONV_WIDTH - 1
D_FF = 2816
REL_BUCKETS = 32
REL_MAX_DIST = 128
PAGE_SIZE = 128
DEPTH = 2
ALPHA = (2 * DEPTH) ** 0.25
LN_EPS = 1e-5

LANES = 128
SUBLANES = 8
V7X_VMEM_LIMIT_BYTES = 56 * 1024 * 1024

LOG2E = math.log2(math.e)
NEG_INF = float("-inf")
M_FLOOR = -3.0e38
INT_MIN = -(2 ** 31)

BQ = 256
CH = 128
TK = 1024
IDX_GROUP = 4
QK_ROWS = 128
PV_KEYS = 256
BAND = CH + BQ

S1_PAGES = 32
S3_PAGES = 8
SELECT_SEQS = 8


def _bucket_lower_bounds():
    max_exact = REL_BUCKETS // 2
    lows = [None] * REL_BUCKETS
    for d in range(0, REL_MAX_DIST + 1):
        if d < max_exact:
            b = d
        else:
            b = max_exact + int(math.log(d / max_exact) / math.log(REL_MAX_DIST / max_exact)
                                * (REL_BUCKETS - max_exact))
            b = min(b, REL_BUCKETS - 1)
        if lows[b] is None:
            lows[b] = d
    nxt = REL_MAX_DIST
    for b in range(REL_BUCKETS - 1, -1, -1):
        if lows[b] is None:
            lows[b] = nxt
        nxt = lows[b]
    return lows


BUCKET_LOW = _bucket_lower_bounds()


def _mm(a, b):
    return jnp.dot(a, b, preferred_element_type=F32)


def _mm_nt(a, b):
    return lax.dot_general(a, b, (((1,), (1,)), ((), ())), preferred_element_type=F32)


def _layer_norm(y, g, b):
    mu = jnp.mean(y, axis=-1, keepdims=True)
    d = y - mu
    var = jnp.mean(d * d, axis=-1, keepdims=True)
    return d * lax.rsqrt(var + LN_EPS) * g + b


def _params(n_axes):
    return pltpu.CompilerParams(dimension_semantics=("arbitrary",) * n_axes,
                                vmem_limit_bytes=V7X_VMEM_LIMIT_BYTES)


def _resident(shape):
    nd = len(shape)
    return pl.BlockSpec(shape, lambda *_: (0,) * nd, pipeline_mode=pl.Buffered(1))


def _row_tile(m):
    for t in (512, 256, 128, 64, 32, 16, 8):
        if m % t == 0:
            return t
    raise ValueError(f"row count {m} is not a multiple of 8")


def _bias_from_dist(dist, rb_ref, h):
    bias = jnp.full(dist.shape, rb_ref[0, h] * LOG2E, F32)
    for b in range(1, REL_BUCKETS):
        bias = jnp.where(dist >= BUCKET_LOW[b], rb_ref[b, h] * LOG2E, bias)
    return bias


def _key_to_f32(k):
    bits = jnp.where(k < 0, k ^ jnp.int32(0x7FFFFFFF), k)
    return lax.bitcast_convert_type(bits, F32)


def _key_hi16(x):
    bits = lax.bitcast_convert_type(x, I32)
    key = bits ^ ((bits >> 31) & jnp.int32(0x7FFFFFFF))
    return (key >> 16).astype(jnp.int16)


PACKED_ROWS = 2 * SUBLANES
KEY_NEG_INF = (0xFF800000 ^ 0x7FFFFFFF) - (1 << 32)
NEG_INF_HI16 = KEY_NEG_INF >> 16


def _glu_kernel(x_ref, w_ref, b_ref, u_ref):
    d = u_ref.shape[-1]
    xb = x_ref[...].astype(BF16)
    a = _mm(xb, w_ref[:, :d]) + b_ref[:, :d]
    g = _mm(xb, w_ref[:, d:]) + b_ref[:, d:]
    u_ref[...] = a * jax.nn.sigmoid(g)


def glu(x, w_bf, b):
    m, d = x.shape
    tm = _row_tile(m)
    return pl.pallas_call(
        _glu_kernel,
        grid=(m // tm,),
        in_specs=[pl.BlockSpec((tm, d), lambda i: (i, 0)), _resident((d, 2 * d)), _resident((1, 2 * d))],
        out_specs=pl.BlockSpec((tm, d), lambda i: (i, 0)),
        out_shape=jax.ShapeDtypeStruct((m, d), F32),
        compiler_params=_params(1),
        name="glu",
    )(x, w_bf, b.reshape(1, 2 * d))


CONV_HALO = 32
CONV_ROWS = 64


def _conv_prompt_kernel(u_ref, halo_ref, wdw_ref, bdw_ref, g_ref, b_ref, z_ref, win_ref, y_ref):
    i = pl.program_id(0)
    tm = u_ref.shape[0]
    ncb = win_ref.shape[0]
    halo = jnp.where(i > 0, halo_ref[...], 0.0)
    for cb in range(ncb):
        win_ref[cb, 0:CONV_HALO, :] = halo[:, cb * LANES:(cb + 1) * LANES]
        win_ref[cb, CONV_HALO:CONV_HALO + tm, :] = u_ref[:, cb * LANES:(cb + 1) * LANES]
    shift = CONV_HALO - CONV_STATE

    def cb_body(cb, carry):
        w = wdw_ref[cb]
        for r in range(tm // CONV_ROWS):
            acc = jnp.zeros((CONV_ROWS, LANES), F32)
            for j in range(CONV_WIDTH):
                acc = acc + win_ref[cb, pl.ds(r * CONV_ROWS + shift + j, CONV_ROWS), :] * w[j:j + 1, :]
            y_ref[cb, r * CONV_ROWS:(r + 1) * CONV_ROWS, :] = acc
        return carry

    lax.fori_loop(0, ncb, cb_body, 0)
    y = jnp.concatenate([y_ref[cb] for cb in range(ncb)], axis=1) + bdw_ref[...]
    z = _layer_norm(y, g_ref[...], b_ref[...])
    z_ref[...] = (z * jax.nn.sigmoid(z)).astype(BF16)


def conv_prompt(u, w_dw, b_dw, g_n, b_n):
    m, d = u.shape
    tm = _row_tile(m)
    ncb = d // LANES
    wdw = jnp.pad(w_dw, ((0, 1), (0, 0))).reshape(CONV_WIDTH + 1, ncb, LANES).transpose(1, 0, 2)
    hb = tm // CONV_HALO
    return pl.pallas_call(
        _conv_prompt_kernel,
        grid=(m // tm,),
        in_specs=[pl.BlockSpec((tm, d), lambda i: (i, 0)),
                  pl.BlockSpec((CONV_HALO, d), lambda i: (jnp.maximum(i * hb - 1, 0), 0)),
                  _resident((ncb, CONV_WIDTH + 1, LANES)),
                  _resident((1, d)), _resident((1, d)), _resident((1, d))],
        out_specs=pl.BlockSpec((tm, d), lambda i: (i, 0)),
        out_shape=jax.ShapeDtypeStruct((m, d), BF16),
        scratch_shapes=[pltpu.VMEM((ncb, tm + CONV_HALO, LANES), F32), pltpu.VMEM((ncb, tm, LANES), F32)],
        compiler_params=_params(1),
        name="conv_prompt",
    )(u, u, wdw, b_dw.reshape(1, d), g_n.reshape(1, d), b_n.reshape(1, d))


def _conv_sample_kernel(st_ref, u_ref, wdw_ref, bdw_ref, g_ref, b_ref, z_ref, ns_ref, win_ref):
    t = u_ref.shape[0]
    win_ref[0:CONV_STATE, :] = st_ref[...]
    win_ref[CONV_STATE:CONV_STATE + t, :] = u_ref[...]
    acc = jnp.zeros(u_ref.shape, F32)
    for j in range(CONV_WIDTH):
        acc = acc + win_ref[j:j + t, :] * wdw_ref[j:j + 1, :]
    z = _layer_norm(acc + bdw_ref[...], g_ref[...], b_ref[...])
    z_ref[...] = z * jax.nn.sigmoid(z)
    ns_ref[...] = win_ref[t:t + CONV_STATE, :]


def conv_sample(u, state, w_dw, b_dw, g_n, b_n):
    b, t, d = u.shape
    return pl.pallas_call(
        _conv_sample_kernel,
        grid=(b,),
        in_specs=[pl.BlockSpec((None, CONV_STATE, d), lambda i: (i, 0, 0)),
                  pl.BlockSpec((None, t, d), lambda i: (i, 0, 0)),
                  _resident((CONV_WIDTH + 1, d)), _resident((1, d)), _resident((1, d)), _resident((1, d))],
        out_specs=[pl.BlockSpec((None, t, d), lambda i: (i, 0, 0)),
                   pl.BlockSpec((None, CONV_STATE, d), lambda i: (i, 0, 0))],
        out_shape=[jax.ShapeDtypeStruct((b, t, d), F32), jax.ShapeDtypeStruct((b, CONV_STATE, d), F32)],
        scratch_shapes=[pltpu.VMEM((CONV_STATE + t + 2, d), F32)],
        compiler_params=_params(1),
        name="conv_sample",
    )(state, u, jnp.pad(w_dw, ((0, 1), (0, 0))), b_dw.reshape(1, d), g_n.reshape(1, d), b_n.reshape(1, d))


def _proj_res_ln_kernel(z_ref, w_ref, bias_ref, x_ref, g_ref, b_ref, o_ref):
    m = _mm(z_ref[...].astype(BF16), w_ref[...]) + bias_ref[...]
    o_ref[...] = _layer_norm(ALPHA * x_ref[...] + m, g_ref[...], b_ref[...])


def proj_res_ln(z, w_bf, bias, x, g, b):
    m, d = x.shape
    k = z.shape[1]
    tm = _row_tile(m)
    return pl.pallas_call(
        _proj_res_ln_kernel,
        grid=(m // tm,),
        in_specs=[pl.BlockSpec((tm, k), lambda i: (i, 0)), _resident((k, d)), _resident((1, d)),
                  pl.BlockSpec((tm, d), lambda i: (i, 0)), _resident((1, d)), _resident((1, d))],
        out_specs=pl.BlockSpec((tm, d), lambda i: (i, 0)),
        out_shape=jax.ShapeDtypeStruct((m, d), F32),
        compiler_params=_params(1),
        name="proj_res_ln",
    )(z, w_bf, bias.reshape(1, d), x, g.reshape(1, d), b.reshape(1, d))


FFN_CHUNK = 256


def _ffn_kernel(x_ref, win_ref, wout_ref, g_ref, b_ref, o_ref):
    x = x_ref[...]
    xb = x.astype(BF16)
    dff = wout_ref.shape[0]
    acc = jnp.zeros(x.shape, F32)
    for c in range(dff // FFN_CHUNK):
        lo = c * FFN_CHUNK
        gate = _mm(xb, win_ref[:, lo:lo + FFN_CHUNK])
        up = _mm(xb, win_ref[:, dff + lo:dff + lo + FFN_CHUNK])
        act = (gate * jax.nn.sigmoid(gate) * up).astype(BF16)
        acc = acc + _mm(act, wout_ref[lo:lo + FFN_CHUNK, :])
    o_ref[...] = _layer_norm(ALPHA * x + acc, g_ref[...], b_ref[...])


def ffn(x, win_bf, wout_bf, g, b):
    m, d = x.shape
    dff = wout_bf.shape[0]
    tm = _row_tile(m)
    return pl.pallas_call(
        _ffn_kernel,
        grid=(m // tm,),
        in_specs=[pl.BlockSpec((tm, d), lambda i: (i, 0)), _resident((d, 2 * dff)), _resident((dff, d)),
                  _resident((1, d)), _resident((1, d))],
        out_specs=pl.BlockSpec((tm, d), lambda i: (i, 0)),
        out_shape=jax.ShapeDtypeStruct((m, d), F32),
        compiler_params=_params(1),
        name="ffn",
    )(x, win_bf, wout_bf, g.reshape(1, d), b.reshape(1, d))


Q_SCALE = HEAD_DIM ** -0.5 * LOG2E
W_SCALE = IDX_HEADS ** -0.5 * IDX_DIM ** -0.5
QI_PAD = IDX_HEADS * LANES


def _split_attn_weights(w_in):
    d = D_MODEL
    o3 = 3 * d
    o4 = o3 + IDX_HEADS * IDX_DIM
    o5 = o4 + IDX_DIM
    wq, wk, wv = w_in[:, :d], w_in[:, d:2 * d], w_in[:, 2 * d:o3]
    wqi = w_in[:, o3:o4].reshape(d, IDX_HEADS, IDX_DIM)
    wqi = jnp.pad(wqi, ((0, 0), (0, 0), (0, LANES - IDX_DIM))).reshape(d, QI_PAD)
    wkw = jnp.pad(w_in[:, o4:], ((0, 0), (0, LANES - IDX_DIM - IDX_HEADS)))
    return tuple(a.astype(BF16) for a in (wq, wk, wv, wqi, wkw))


def _attn_proj_t_kernel(x_ref, wk_ref, wv_ref, wkw_ref, wqT_ref, wvT_ref, wqiT_ref, wwT_ref,
                        k32_ref, v32_ref, kiw_ref, qT_ref, kbf_ref, vT_ref, qiT_ref, kip_ref, wT_ref):
    xb = x_ref[...].astype(BF16)
    k = _mm(xb, wk_ref[...])
    k32_ref[...] = k
    kbf_ref[...] = k.astype(BF16)
    v32_ref[...] = _mm(xb, wv_ref[...])
    kiw = _mm(xb, wkw_ref[...])
    kiw_ref[...] = kiw
    lane = lax.broadcasted_iota(I32, kiw.shape, 1)
    kip_ref[...] = jnp.where(lane < IDX_DIM, kiw, 0.0).astype(BF16)
    qT_ref[...] = (_mm_nt(wqT_ref[...], xb) * Q_SCALE).astype(BF16)
    vT_ref[...] = _mm_nt(wvT_ref[...], xb).astype(BF16)
    qiT_ref[...] = _mm_nt(wqiT_ref[...], xb).astype(BF16)
    wT_ref[...] = _mm_nt(wwT_ref[...], xb) * W_SCALE


def attn_proj_t(x, w_in):
    m, d = x.shape
    tm = _row_tile(m)
    wq, wk, wv, wqi, wkw = _split_attn_weights(w_in)
    o5 = 3 * d + IDX_HEADS * IDX_DIM + IDX_DIM
    wwT = jnp.pad(w_in[:, o5:].T, ((0, 16 - IDX_HEADS), (0, 0))).astype(BF16)
    row = lambda n: pl.BlockSpec((tm, n), lambda i: (i, 0))
    col = lambda n: pl.BlockSpec((n, tm), lambda i: (0, i))
    return pl.pallas_call(
        _attn_proj_t_kernel,
        grid=(m // tm,),
        in_specs=[row(d), _resident((d, d)), _resident((d, d)), _resident((d, LANES)),
                  _resident((d, d)), _resident((d, d)), _resident((QI_PAD, d)), _resident((16, d))],
        out_specs=[row(d), row(d), row(LANES), col(d), row(d), col(d), col(QI_PAD), row(LANES), col(16)],
        out_shape=[jax.ShapeDtypeStruct((m, d), F32), jax.ShapeDtypeStruct((m, d), F32),
                   jax.ShapeDtypeStruct((m, LANES), F32), jax.ShapeDtypeStruct((d, m), BF16),
                   jax.ShapeDtypeStruct((m, d), BF16), jax.ShapeDtypeStruct((d, m), BF16),
                   jax.ShapeDtypeStruct((QI_PAD, m), BF16), jax.ShapeDtypeStruct((m, LANES), BF16),
                   jax.ShapeDtypeStruct((16, m), F32)],
        compiler_params=_params(1),
        name="attn_proj_t",
    )(x, wk, wv, wkw, wq.T, wv.T, wqi.T, wwT)


def _attn_proj_s_kernel(x_ref, wq_ref, wk_ref, wv_ref, wqi_ref, wkw_ref,
                        q_ref, k32_ref, v32_ref, qi_ref, kiw_ref):
    xb = x_ref[...].astype(BF16)
    q_ref[...] = (_mm(xb, wq_ref[...]) * Q_SCALE).astype(BF16)
    k32_ref[...] = _mm(xb, wk_ref[...])
    v32_ref[...] = _mm(xb, wv_ref[...])
    qi_ref[...] = _mm(xb, wqi_ref[...]).astype(BF16)
    kiw_ref[...] = _mm(xb, wkw_ref[...])


def attn_proj_s(x, w_in):
    m, d = x.shape
    tm = _row_tile(m)
    wq, wk, wv, wqi, wkw = _split_attn_weights(w_in)
    row = lambda n: pl.BlockSpec((tm, n), lambda i: (i, 0))
    return pl.pallas_call(
        _attn_proj_s_kernel,
        grid=(m // tm,),
        in_specs=[row(d), _resident((d, d)), _resident((d, d)), _resident((d, d)),
                  _resident((d, QI_PAD)), _resident((d, LANES))],
        out_specs=[row(d), row(d), row(d), row(QI_PAD), row(LANES)],
        out_shape=[jax.ShapeDtypeStruct((m, d), BF16), jax.ShapeDtypeStruct((m, d), F32),
                   jax.ShapeDtypeStruct((m, d), F32), jax.ShapeDtypeStruct((m, QI_PAD), BF16),
                   jax.ShapeDtypeStruct((m, LANES), F32)],
        compiler_params=_params(1),
        name="attn_proj_s",
    )(x, wq, wk, wv, wqi, wkw)


def _prompt_schedule(t):
    qb_of, kt_of, fl = [], [], []
    for qb in range(t // BQ):
        nb = -(-max(qb * BQ - CH, 0) // TK)
        qb_of.append(qb); kt_of.append(0); fl.append(1 | (2 if nb == 0 else 0))
        for kt in range(nb):
            qb_of.append(qb); kt_of.append(kt); fl.append(2 if kt == nb - 1 else 0)
    return (np.asarray(qb_of, np.int32), np.asarray(kt_of, np.int32), np.asarray(fl, np.int32))


def _attn_prompt_kernel(qb_ref, kt_ref, fl_ref,
                        qT_ref, qiT_ref, wT_ref, kip_ref, kb_ref, vTb_ref, kp_ref, kd_ref, vTp_ref, vTd_ref,
                        rb_ref, o_ref,
                        s_ref, h_ref, thr_ref, bb_ref, m_ref, l_ref, acc_ref, madd_ref, *, k_sel, idx_bits):
    step = pl.program_id(0)
    qb = qb_ref[step]
    kt = kt_ref[step]
    flags = fl_ref[step]
    is_band = (flags & 1) != 0
    is_last = (flags & 2) != 0
    q0 = qb * BQ
    nch = 2 * (qb + 1)
    cband = 2 * qb - 1
    kf = float(k_sel)

    @pl.when(step == 0)
    def _build_band_bias():
        r = lax.broadcasted_iota(I32, (BAND, BQ), 0)
        j = lax.broadcasted_iota(I32, (BAND, BQ), 1)
        dist = CH + j - r
        for h in range(N_HEADS):
            bb_ref[h] = _bias_from_dist(dist, rb_ref, h)

    def head_cols(h):
        return slice(h * HEAD_DIM, (h + 1) * HEAD_DIM)

    def qk(h, kh):
        n = kh.shape[0] // QK_ROWS
        return jnp.concatenate([_mm(kh[i * QK_ROWS:(i + 1) * QK_ROWS], qT_ref[head_cols(h), :])
                                for i in range(n)], axis=0)

    def softmax_pv(h, lt, vth, madd, bias, cbias):
        hs = head_cols(h)
        if bias is not None:
            lt = lt + bias
        lt = lt + madd
        mx = jnp.max(lt, axis=0, keepdims=True) + cbias
        m_old = m_ref[h:h + 1, :]
        m_new = jnp.maximum(m_old, mx)
        alpha = jnp.exp2(m_old - m_new)
        p = jnp.exp2(lt - (m_new - cbias))
        psum = p[0:SUBLANES]
        for i in range(1, p.shape[0] // SUBLANES):
            psum = psum + p[i * SUBLANES:(i + 1) * SUBLANES]
        l_ref[h] = alpha * l_ref[h] + psum
        pb = p.astype(BF16)
        r = pb.shape[0]
        if r % (2 * PV_KEYS) == 0:
            pv = _mm(vth[:, :r // 2], pb[:r // 2]) + _mm(vth[:, r // 2:], pb[r // 2:])
        else:
            pv = _mm(vth, pb)
        acc_ref[hs, :] = alpha * acc_ref[hs, :] + pv
        m_ref[h:h + 1, :] = m_new

    def attend_heads(keys_of, vals_of, madd, bias_of, cbias_of):
        lt = qk(0, keys_of(0))
        for h in range(N_HEADS):
            lt_next = qk(h + 1, keys_of(h + 1)) if h + 1 < N_HEADS else None
            softmax_pv(h, lt, vals_of(h), madd, bias_of(h), cbias_of(h))
            lt = lt_next

    @pl.when(is_band)
    def _band_step():
        def idx_body(gi, carry):
            rows = IDX_GROUP * CH
            kic = kip_ref[pl.ds(pl.multiple_of(gi * rows, rows), rows), :]
            acc = None
            for h in range(IDX_HEADS):
                sc = _mm(kic, qiT_ref[h * LANES:(h + 1) * LANES, :])
                val = jnp.maximum(sc, 0.0) * wT_ref[h:h + 1, :]
                acc = val if acc is None else acc + val
            acc = jnp.where(acc == 0.0, 0.0, acc)
            hi = _key_hi16(acc)
            for i in range(IDX_GROUP):
                s_ref[gi * IDX_GROUP + i] = acc[i * CH:(i + 1) * CH]
                h_ref[gi * IDX_GROUP + i] = hi[i * CH:(i + 1) * CH]
            return carry

        ngrp = (nch + IDX_GROUP - 1) // IDX_GROUP
        lax.fori_loop(0, ngrp, idx_body, 0)
        rr = lax.broadcasted_iota(I32, (CH, BQ), 0)
        jj = lax.broadcasted_iota(I32, (CH, BQ), 1)
        for i in range(BQ // CH):
            c = 2 * qb + i
            masked = jnp.where(i * CH + rr <= jj, s_ref[c], NEG_INF)
            s_ref[c] = masked
            h_ref[c] = _key_hi16(masked)

        @pl.when(ngrp * IDX_GROUP > nch)
        def _mask_group_tail():
            for c in range(IDX_GROUP - BQ // CH):
                s_ref[nch + c] = jnp.full((CH, BQ), NEG_INF, F32)
                h_ref[nch + c] = jnp.full((CH, BQ), NEG_INF_HI16, jnp.int16)

        def count(pred):
            def body(g, acc):
                for c in range(IDX_GROUP):
                    ci = g * IDX_GROUP + c
                    ind = jnp.where(pred(ci, s_ref[ci]), 1.0, 0.0)
                    for r in range(CH // SUBLANES):
                        acc = acc + ind[r * SUBLANES:(r + 1) * SUBLANES]
                return acc
            acc = lax.fori_loop(0, ngrp, body, jnp.zeros((SUBLANES, BQ), F32))
            return jnp.broadcast_to(jnp.sum(acc, axis=0, keepdims=True), (SUBLANES, BQ))

        def count_ge(cand_key):
            cand = _key_to_f32(cand_key)[0:1, :]
            return count(lambda c, x: x >= cand)

        def count_hi(pred):
            one, zero = jnp.ones((), jnp.int16), jnp.zeros((), jnp.int16)

            def body(g, acc):
                for c in range(IDX_GROUP):
                    ind = jnp.where(pred(h_ref[g * IDX_GROUP + c]), one, zero)
                    for r in range(CH // PACKED_ROWS):
                        acc = acc + ind[r * PACKED_ROWS:(r + 1) * PACKED_ROWS]
                return acc
            acc = lax.fori_loop(0, ngrp, body, jnp.zeros((PACKED_ROWS, BQ), jnp.int16))
            tot = jnp.sum(acc.astype(I32).astype(F32), axis=0, keepdims=True)
            return jnp.broadcast_to(tot, (SUBLANES, BQ))

        def count_hi_ge(cand_hi):
            c16 = jnp.broadcast_to(cand_hi[0:1, :], (PACKED_ROWS, BQ)).astype(jnp.int16)[0:1, :]
            return count_hi(lambda x: x >= c16)

        total = (ngrp * (IDX_GROUP * CH)).astype(F32)

        def search(count_fn, ans, cnt, top_bit, n_bits):
            def cond(carry):
                i, _, cnt = carry
                return jnp.logical_and(i < n_bits, jnp.max(jnp.abs(cnt - kf)) > 0.5)

            def body(carry):
                i, ans, cnt = carry
                cand = ans + jnp.left_shift(jnp.int32(1), top_bit - i)
                c = count_fn(cand)
                ok = jnp.logical_and(c >= kf, cnt != kf)
                return i + 1, jnp.where(ok, cand, ans), jnp.where(ok, c, cnt)

            _, ans, cnt = lax.while_loop(cond, body, (jnp.int32(0), ans, cnt))
            return ans, cnt

        c0 = count_hi(lambda x: x >= jnp.zeros((), jnp.int16))
        ans_hi = jnp.where(c0 >= kf, 0, -(2 ** 15)).astype(I32)
        cnt = jnp.where(c0 >= kf, c0, total)
        ans_hi, cnt = search(count_hi_ge, ans_hi, cnt, 14, 15)
        ans, cnt = search(count_ge, ans_hi * (2 ** 16), cnt, 15, 16)
        none = ans <= KEY_NEG_INF
        thr = jnp.where(none, NEG_INF, _key_to_f32(ans))
        thr_ref[...] = thr

        has = jnp.logical_and(cnt > kf, jnp.logical_not(none))
        any_ties = jnp.max(jnp.where(has, 1.0, 0.0)) > 0.5

        @pl.when(any_ties)
        def _resolve_ties():
            t1 = thr[0:1, :]
            n_gt = count(lambda c, x: x > t1)
            need = jnp.where(has, kf - n_gt, 3.0e38)
            rows = lax.broadcasted_iota(I32, (CH, BQ), 0)

            def j_body(i, j):
                cj = j + jnp.left_shift(jnp.int32(1), idx_bits - 1 - i)
                c1 = cj[0:1, :]
                f = count(lambda c, x: jnp.logical_and(x == t1, c * CH + rows < c1))
                return jnp.where(f < need, cj, j)

            jstar = lax.fori_loop(0, idx_bits, j_body, jnp.zeros((SUBLANES, BQ), I32))
            j1 = jstar[0:1, :]
            h1 = has[0:1, :]

            def demote(c, carry):
                x = s_ref[c]
                drop = jnp.logical_and(jnp.logical_and(x == t1, c * CH + rows > j1), h1)
                s_ref[c] = jnp.where(drop, NEG_INF, x)
                return carry

            lax.fori_loop(0, nch, demote, 0)

        m_ref[...] = jnp.full(m_ref.shape, M_FLOOR, F32)
        l_ref[...] = jnp.zeros(l_ref.shape, F32)
        acc_ref[...] = jnp.zeros(acc_ref.shape, F32)
        thr1 = thr_ref[0:1, :]
        cprev = jnp.maximum(cband, 0)
        sc = jnp.concatenate([s_ref[cprev], s_ref[2 * qb], s_ref[2 * qb + 1]], axis=0)
        r = lax.broadcasted_iota(I32, (BAND, BQ), 0)
        j = lax.broadcasted_iota(I32, (BAND, BQ), 1)
        ok = jnp.logical_and(sc >= thr1, r - CH <= j)
        ok = jnp.logical_and(ok, jnp.logical_or(r >= CH, qb > 0))
        madd = jnp.where(ok, 0.0, NEG_INF)
        attend_heads(lambda h: jnp.concatenate([kp_ref[:, head_cols(h)], kd_ref[:, head_cols(h)]], axis=0),
                     lambda h: jnp.concatenate([vTp_ref[head_cols(h), :], vTd_ref[head_cols(h), :]], axis=1),
                     madd, lambda h: bb_ref[h], lambda h: 0.0)

    @pl.when(jnp.logical_not(is_band))
    def _bulk_step():
        thr1 = thr_ref[0:1, :]
        for i in range(TK // CH):
            c = kt * (TK // CH) + i
            x = s_ref[jnp.minimum(c, cband - 1)]
            sel = jnp.where(x >= thr1, 0.0, NEG_INF)
            madd_ref[i * CH:(i + 1) * CH, :] = jnp.where(c < cband, sel, NEG_INF)
        attend_heads(lambda h: kb_ref[:, head_cols(h)], lambda h: vTb_ref[head_cols(h), :], madd_ref[...],
                     lambda h: None, lambda h: rb_ref[REL_BUCKETS - 1, h] * LOG2E)

    @pl.when(is_last)
    def _finalize():
        for h in range(N_HEADS):
            hs = slice(h * HEAD_DIM, (h + 1) * HEAD_DIM)
            lsum = jnp.sum(l_ref[h], axis=0, keepdims=True)
            o_ref[:, hs] = (acc_ref[hs, :] / lsum).T.astype(BF16)


def attn_prompt(qT, qiT, wT, kip, kbf, vT, rel_bias):
    d, t = qT.shape
    k_sel = min(TOPK_MAX, t // 4)
    qb_of, kt_of, fl = _prompt_schedule(t)
    n_steps = len(qb_of)
    tk = min(TK, t)
    qcol = lambda n: pl.BlockSpec((n, BQ), lambda s, qb, kt, fl: (0, qb[s]))
    in_specs = [
        qcol(d), qcol(QI_PAD), qcol(16),
        pl.BlockSpec((t, LANES), lambda s, qb, kt, fl: (0, 0), pipeline_mode=pl.Buffered(1)),
        pl.BlockSpec((tk, d), lambda s, qb, kt, fl: (kt[s], 0)),
        pl.BlockSpec((d, tk), lambda s, qb, kt, fl: (0, kt[s])),
        pl.BlockSpec((CH, d), lambda s, qb, kt, fl: (jnp.maximum(2 * qb[s] - 1, 0), 0)),
        pl.BlockSpec((BQ, d), lambda s, qb, kt, fl: (qb[s], 0)),
        pl.BlockSpec((d, CH), lambda s, qb, kt, fl: (0, jnp.maximum(2 * qb[s] - 1, 0))),
        pl.BlockSpec((d, BQ), lambda s, qb, kt, fl: (0, qb[s])),
        pl.BlockSpec(memory_space=pltpu.SMEM),
    ]
    grid_spec = pltpu.PrefetchScalarGridSpec(
        num_scalar_prefetch=3,
        grid=(n_steps,),
        in_specs=in_specs,
        out_specs=pl.BlockSpec((BQ, d), lambda s, qb, kt, fl: (qb[s], 0)),
        scratch_shapes=[
            pltpu.VMEM((t // CH, CH, BQ), F32),
            pltpu.VMEM((t // CH, CH, BQ), jnp.int16),
            pltpu.VMEM((SUBLANES, BQ), F32),
            pltpu.VMEM((N_HEADS, BAND, BQ), F32),
            pltpu.VMEM((N_HEADS, BQ), F32),
            pltpu.VMEM((N_HEADS, SUBLANES, BQ), F32),
            pltpu.VMEM((d, BQ), F32),
            pltpu.VMEM((tk, BQ), F32),
        ],
    )
    kern = functools.partial(_attn_prompt_kernel, k_sel=k_sel, idx_bits=max(t.bit_length() - 1, 1))
    return pl.pallas_call(
        kern,
        grid_spec=grid_spec,
        out_shape=jax.ShapeDtypeStruct((t, d), BF16),
        compiler_params=_params(1),
        name="attn_prompt",
    )(jnp.asarray(qb_of), jnp.asarray(kt_of), jnp.asarray(fl),
      qT, qiT, wT, kip, kbf, vT, kbf, kbf, vT, vT, rel_bias)


def _sample_scores_kernel(pt_ref, qi_ref, w_ref, *rest, n_groups):
    pages = rest[:S1_PAGES]
    kin_ref, o_ref, kbuf_ref = rest[S1_PAGES:]
    b = pl.program_id(0)
    g = pl.program_id(1)
    t = o_ref.shape[1]

    @pl.when(jnp.logical_and(b == 0, g == 0))
    def _zero_pad_rows():
        kbuf_ref[...] = jnp.zeros(kbuf_ref.shape, BF16)

    def head_sum(val):
        s = val[0:t]
        for h in range(1, IDX_HEADS):
            s = s + val[h * t:(h + 1) * t]
        return s

    wcol = w_ref[:, 0:1]

    @pl.when(g < n_groups)
    def _past_pages():
        for i in range(S1_PAGES):
            kbuf_ref[0:IDX_DIM, i * PAGE_SIZE:(i + 1) * PAGE_SIZE] = pages[i][...].astype(BF16)
        s = head_sum(jnp.maximum(_mm(qi_ref[...], kbuf_ref[...]), 0.0) * wcol)
        for i in range(S1_PAGES):
            o_ref[i] = s[:, i * PAGE_SIZE:(i + 1) * PAGE_SIZE]

    @pl.when(g == n_groups)
    def _new_keys():
        s = head_sum(jnp.maximum(_mm_nt(qi_ref[...], kin_ref[...]), 0.0) * wcol)
        row = lax.broadcasted_iota(I32, s.shape, 0)
        lane = lax.broadcasted_iota(I32, s.shape, 1)
        o_ref[0] = jnp.where(lane <= row, s, NEG_INF)
        for i in range(1, S1_PAGES):
            o_ref[i] = jnp.full(s.shape, NEG_INF, F32)


def sample_scores(page_table, qi_ht, w_ht, cache_kidx, layer, ki_new):
    b, n_pages = page_table.shape
    t = qi_ht.shape[1] // IDX_HEADS
    n_groups = n_pages // S1_PAGES
    page_spec = lambda i: pl.BlockSpec(
        (None, None, IDX_DIM, PAGE_SIZE),
        lambda bi, g, pt: (layer, pt[bi, jnp.minimum(g, n_groups - 1) * S1_PAGES + i], 0, 0))
    grid_spec = pltpu.PrefetchScalarGridSpec(
        num_scalar_prefetch=1,
        grid=(b, n_groups + 1),
        in_specs=[pl.BlockSpec((None, IDX_HEADS * t, LANES), lambda bi, g, pt: (bi, 0, 0)),
                  pl.BlockSpec((None, IDX_HEADS * t, LANES), lambda bi, g, pt: (bi, 0, 0))]
                 + [page_spec(i) for i in range(S1_PAGES)]
                 + [pl.BlockSpec((None, PAGE_SIZE, LANES), lambda bi, g, pt: (bi, 0, 0))],
        out_specs=pl.BlockSpec((None, S1_PAGES, t, LANES), lambda bi, g, pt: (bi, g, 0, 0)),
        scratch_shapes=[pltpu.VMEM((LANES, S1_PAGES * PAGE_SIZE), BF16)],
    )
    kidx_t = jnp.swapaxes(cache_kidx, 2, 3)
    return pl.pallas_call(
        functools.partial(_sample_scores_kernel, n_groups=n_groups),
        grid_spec=grid_spec,
        out_shape=jax.ShapeDtypeStruct((b, (n_groups + 1) * S1_PAGES, t, LANES), F32),
        compiler_params=_params(2),
        name="sample_scores",
    )(page_table, qi_ht, w_ht, *([kidx_t] * S1_PAGES), ki_new)


def _sample_select_kernel(s_ref, o_ref, *, k_sel, past, idx_bits):
    x = s_ref[...]
    nb, nck, t, _ = x.shape
    kf = float(k_sel)
    cidx = lax.broadcasted_iota(I32, x.shape, 1)
    row = lax.broadcasted_iota(I32, x.shape, 2)
    lane = lax.broadcasted_iota(I32, x.shape, 3)
    idx = cidx * LANES + lane

    def count(pred):
        per_lane = jnp.sum(jnp.where(pred, 1.0, 0.0), axis=1)
        return jnp.broadcast_to(jnp.sum(per_lane, axis=2, keepdims=True), (nb, t, LANES))

    def count_ge(cand_key):
        return count(x >= _key_to_f32(cand_key)[:, None])

    total = float(nck * LANES)
    c0 = count(x >= 0.0)
    ans = jnp.where(c0 >= kf, 0, INT_MIN).astype(I32)
    cnt = jnp.where(c0 >= kf, c0, total)

    def bit_cond(carry):
        i, _, cnt = carry
        return jnp.logical_and(i < 31, jnp.max(jnp.abs(cnt - kf)) > 0.5)

    def bit_body(carry):
        i, ans, cnt = carry
        cand = ans + jnp.left_shift(jnp.int32(1), 30 - i)
        c = count_ge(cand)
        ok = jnp.logical_and(c >= kf, cnt != kf)
        return i + 1, jnp.where(ok, cand, ans), jnp.where(ok, c, cnt)

    _, ans, cnt = lax.while_loop(bit_cond, bit_body, (jnp.int32(0), ans, cnt))
    none = ans == INT_MIN
    thr = jnp.where(none, NEG_INF, _key_to_f32(ans))
    has = jnp.logical_and(cnt > kf, jnp.logical_not(none))
    any_ties = jnp.max(jnp.where(has, 1.0, 0.0)) > 0.5
    new = idx - past
    valid = jnp.logical_or(idx < past, jnp.logical_and(new < t, new <= row))
    keep = jnp.logical_and(x >= thr[:, None], valid)

    @pl.when(jnp.logical_not(any_ties))
    def _no_ties():
        o_ref[...] = jnp.where(keep, 0.0, NEG_INF)

    @pl.when(any_ties)
    def _resolve_ties():
        n_gt = count(x > thr[:, None])
        need = jnp.where(has, kf - n_gt, 3.0e38)
        eq = x == thr[:, None]

        def j_body(i, j):
            cj = j + jnp.left_shift(jnp.int32(1), idx_bits - 1 - i)
            f = count(jnp.logical_and(eq, idx < cj[:, None]))
            return jnp.where(f < need, cj, j)

        jstar = lax.fori_loop(0, idx_bits, j_body, jnp.zeros((nb, t, LANES), I32))
        drop = jnp.logical_and(jnp.logical_and(eq, idx > jstar[:, None]), has[:, None])
        o_ref[...] = jnp.where(jnp.logical_and(keep, jnp.logical_not(drop)), 0.0, NEG_INF)


def sample_select(scores, past, k_sel):
    b, nck, t, _ = scores.shape
    nb = math.gcd(b, SELECT_SEQS)
    spec = pl.BlockSpec((nb, nck, t, LANES), lambda i: (i, 0, 0, 0))
    kern = functools.partial(_sample_select_kernel, k_sel=k_sel, past=past,
                             idx_bits=(nck * LANES - 1).bit_length())
    return pl.pallas_call(
        kern, grid=(b // nb,), in_specs=[spec], out_specs=spec,
        out_shape=jax.ShapeDtypeStruct(scores.shape, F32),
        compiler_params=_params(1),
        name="sample_select",
    )(scores)


def _sample_attn_kernel(pt_ref, q_ref, madd_ref, *rest, n_groups, t):
    kpages = rest[:S3_PAGES]
    vpages = rest[S3_PAGES:2 * S3_PAGES]
    (kn_ref, vn_ref, rb_ref, o_ref,
     kcat_ref, vcat_ref, m_ref, l_ref, acc_ref, bconst_ref, blast_ref, bnew_ref) = rest[2 * S3_PAGES:]
    b = pl.program_id(0)
    g = pl.program_id(1)
    rows = N_HEADS * t

    @pl.when(jnp.logical_and(b == 0, g == 0))
    def _build_bias_tables():
        tq = lax.broadcasted_iota(I32, (t, LANES), 0)
        ln = lax.broadcasted_iota(I32, (t, LANES), 1)
        for h in range(N_HEADS):
            hs = slice(h * t, (h + 1) * t)
            bconst_ref[hs, :] = jnp.full((t, LANES), rb_ref[REL_BUCKETS - 1, h] * LOG2E, F32)
            blast_ref[hs, :] = _bias_from_dist(PAGE_SIZE + tq - ln, rb_ref, h)
            bnew_ref[hs, :] = _bias_from_dist(tq - ln, rb_ref, h)

    @pl.when(g == 0)
    def _init():
        m_ref[...] = jnp.full(m_ref.shape, M_FLOOR, F32)
        l_ref[...] = jnp.zeros(l_ref.shape, F32)
        acc_ref[...] = jnp.zeros(acc_ref.shape, F32)

    def tile_heads(m8):
        return jnp.concatenate([m8] * N_HEADS, axis=0)

    def attend(kk, vv, bias, madd):
        lg = _mm_nt(q_ref[...], kk) + bias + madd
        mx = jnp.max(lg, axis=1, keepdims=True)
        m_old = m_ref[...]
        m_new = jnp.maximum(m_old, mx)
        alpha = jnp.exp2(m_old - m_new)
        p = jnp.exp2(lg - m_new[:, 0:1])
        l_ref[...] = alpha * l_ref[...] + jnp.sum(p, axis=1, keepdims=True)
        acc_ref[...] = alpha[:, 0:1] * acc_ref[...] + _mm(p.astype(BF16), vv)
        m_ref[...] = m_new

    @pl.when(g < n_groups)
    def _cached_pages():
        for i in range(S3_PAGES):
            rs = slice(i * PAGE_SIZE, (i + 1) * PAGE_SIZE)
            for h in range(N_HEADS):
                hs = slice(h * HEAD_DIM, (h + 1) * HEAD_DIM)
                head_rows = pl.ds(h, PAGE_SIZE, stride=N_HEADS)
                kcat_ref[rs, hs] = kpages[i][head_rows, :].astype(BF16)
                vcat_ref[rs, hs] = vpages[i][head_rows, :].astype(BF16)
        last = jnp.where(g == n_groups - 1, blast_ref[...], bconst_ref[...])
        bias = jnp.concatenate([bconst_ref[...]] * (S3_PAGES - 1) + [last], axis=1)
        madd = jnp.concatenate([tile_heads(madd_ref[i]) for i in range(S3_PAGES)], axis=1)
        attend(kcat_ref[...], vcat_ref[...], bias, madd)

    @pl.when(g == n_groups)
    def _new_keys_and_finish():
        attend(kn_ref[...], vn_ref[...], bnew_ref[...], tile_heads(madd_ref[0]))
        for h in range(N_HEADS):
            hs = slice(h * HEAD_DIM, (h + 1) * HEAD_DIM)
            rs = slice(h * t, (h + 1) * t)
            o_ref[:, hs] = acc_ref[rs, hs] / l_ref[rs, 0:1]


def sample_attn(page_table, q_bd, madd, cache_k, cache_v, layer, k_new, v_new, rel_bias):
    b, n_pages = page_table.shape
    rows, d = q_bd.shape[1:]
    t = rows // N_HEADS
    n_groups = n_pages // S3_PAGES
    page_spec = lambda i: pl.BlockSpec(
        (None, None, PAGE_SIZE * N_HEADS, HEAD_DIM),
        lambda bi, g, pt: (layer, pt[bi, jnp.minimum(g, n_groups - 1) * S3_PAGES + i], 0, 0))
    per_b = lambda shape: pl.BlockSpec((None,) + shape, lambda bi, g, pt: (bi,) + (0,) * len(shape))
    grid_spec = pltpu.PrefetchScalarGridSpec(
        num_scalar_prefetch=1,
        grid=(b, n_groups + 1),
        in_specs=[per_b((rows, d)),
                  pl.BlockSpec((None, S3_PAGES, t, LANES), lambda bi, g, pt: (bi, g, 0, 0))]
                 + [page_spec(i) for i in range(S3_PAGES)] * 2
                 + [per_b((PAGE_SIZE, d)), per_b((PAGE_SIZE, d)), pl.BlockSpec(memory_space=pltpu.SMEM)],
        out_specs=per_b((t, d)),
        scratch_shapes=[pltpu.VMEM((S3_PAGES * PAGE_SIZE, d), BF16), pltpu.VMEM((S3_PAGES * PAGE_SIZE, d), BF16),
                        pltpu.VMEM((rows, LANES), F32), pltpu.VMEM((rows, LANES), F32),
                        pltpu.VMEM((rows, d), F32),
                        pltpu.VMEM((rows, LANES), F32), pltpu.VMEM((rows, LANES), F32),
                        pltpu.VMEM((rows, LANES), F32)],
    )
    return pl.pallas_call(
        functools.partial(_sample_attn_kernel, n_groups=n_groups, t=t),
        grid_spec=grid_spec,
        out_shape=jax.ShapeDtypeStruct((b, t, d), F32),
        compiler_params=_params(2),
        name="sample_attn",
    )(page_table, q_bd, madd, *([cache_k] * S3_PAGES), *([cache_v] * S3_PAGES), k_new, v_new, rel_bias)


def attn_sample(x, cache_k, cache_v, cache_kidx, page_table, layer, w_in, rel_bias):
    b, t, d = x.shape
    past = page_table.shape[1] * PAGE_SIZE
    k_sel = min(TOPK_MAX, (past + t) // 4)
    q, k32, v32, qi, kiw = attn_proj_s(x.reshape(b * t, d), w_in)
    qi_ht = qi.reshape(b, t, IDX_HEADS, LANES).transpose(0, 2, 1, 3).reshape(b, IDX_HEADS * t, LANES)
    wi = kiw[:, IDX_DIM:IDX_DIM + IDX_HEADS] * W_SCALE
    w_ht = jnp.broadcast_to(wi.reshape(b, t, IDX_HEADS).transpose(0, 2, 1).reshape(b, IDX_HEADS * t, 1),
                            (b, IDX_HEADS * t, LANES))
    head_of_col = jnp.arange(d, dtype=I32) // HEAD_DIM
    q_bd = jnp.where(head_of_col[None, None, None, :] == jnp.arange(N_HEADS, dtype=I32)[None, :, None, None],
                     q.reshape(b, 1, t, d), jnp.zeros((), BF16)).reshape(b, N_HEADS * t, d)
    pad_rows = lambda a: jnp.pad(a.reshape(b, t, -1), ((0, 0), (0, PAGE_SIZE - t), (0, 0))).astype(BF16)
    ki_new = pad_rows(jnp.pad(kiw[:, :IDX_DIM], ((0, 0), (0, LANES - IDX_DIM))))
    scores = sample_scores(page_table, qi_ht, w_ht, cache_kidx, layer, ki_new)
    madd = sample_select(scores, past, k_sel)
    page_rows = lambda c: c.reshape(c.shape[0], c.shape[1], PAGE_SIZE * N_HEADS, HEAD_DIM)
    out = sample_attn(page_table, q_bd, madd, page_rows(cache_k), page_rows(cache_v), layer,
                      pad_rows(k32), pad_rows(v32), rel_bias)
    return out.reshape(b * t, d), k32, v32, kiw[:, :IDX_DIM]


def kernel(x_prompt, x_sample, state_conv, cache_k, cache_v, cache_kidx, page_table, rel_bias,
           w_pw1, b_pw1, w_dw, b_dw, conv_norm_g, conv_norm_b, w_pw2, b_pw2,
           w_attn_in, w_attn_out, w_ffn_in, w_ffn_out, ln_mix_g, ln_mix_b, ln_ffn_g, ln_ffn_b):
    bp, tp, d = x_prompt.shape
    bs, ts, _ = x_sample.shape
    assert bp == 1 and d == D_MODEL and tp % TK == 0
    xp = x_prompt.reshape(tp, d)
    xs = x_sample.reshape(bs * ts, d)
    bf = lambda a: a.astype(BF16)

    w1, w2 = bf(w_pw1[0]), bf(w_pw2[0])
    up = glu(xp, w1, b_pw1[0])
    us = glu(xs, w1, b_pw1[0])
    zp = conv_prompt(up, w_dw[0], b_dw[0], conv_norm_g[0], conv_norm_b[0])
    zs, conv_s = conv_sample(us.reshape(bs, ts, d), state_conv[0], w_dw[0], b_dw[0],
                             conv_norm_g[0], conv_norm_b[0])
    conv_p = up[tp - CONV_STATE:]
    xp = proj_res_ln(zp, w2, b_pw2[0], xp, ln_mix_g[0], ln_mix_b[0])
    xs = proj_res_ln(zs.reshape(bs * ts, d), w2, b_pw2[0], xs, ln_mix_g[0], ln_mix_b[0])
    wf_in, wf_out = bf(w_ffn_in[0]), bf(w_ffn_out[0])
    xp = ffn(xp, wf_in, wf_out, ln_ffn_g[0], ln_ffn_b[0])
    xs = ffn(xs, wf_in, wf_out, ln_ffn_g[0], ln_ffn_b[0])

    w_o = bf(w_attn_out[0])
    no_bias = jnp.zeros((d,), F32)
    kp32, vp32, kiwp, qT, kbf, vT, qiT, kip, wT = attn_proj_t(xp, w_attn_in[0])
    ap = attn_prompt(qT, qiT, wT, kip, kbf, vT, rel_bias)
    xp = proj_res_ln(ap, w_o, no_bias, xp, ln_mix_g[1], ln_mix_b[1])
    a_s, ks32, vs32, kis = attn_sample(xs.reshape(bs, ts, d), cache_k, cache_v, cache_kidx, page_table, 0,
                                       w_attn_in[0], rel_bias)
    xs = proj_res_ln(a_s, w_o, no_bias, xs, ln_mix_g[1], ln_mix_b[1])
    wf_in, wf_out = bf(w_ffn_in[1]), bf(w_ffn_out[1])
    xp = ffn(xp, wf_in, wf_out, ln_ffn_g[1], ln_ffn_b[1])
    xs = ffn(xs, wf_in, wf_out, ln_ffn_g[1], ln_ffn_b[1])

    return (xp.reshape(1, tp, d), xs.reshape(bs, ts, d),
            conv_p.reshape(1, 1, CONV_STATE, d), conv_s.reshape(1, bs, CONV_STATE, d),
            kp32.reshape(1, 1, tp, N_HEADS, HEAD_DIM), vp32.reshape(1, 1, tp, N_HEADS, HEAD_DIM),
            kiwp[:, :IDX_DIM].reshape(1, 1, tp, IDX_DIM),
            ks32.reshape(1, bs, ts, N_HEADS, HEAD_DIM), vs32.reshape(1, bs, ts, N_HEADS, HEAD_DIM),
            kis.reshape(1, bs, ts, IDX_DIM))
```

```python
import functools
import math

import numpy as np
import jax
import jax.numpy as jnp
from jax import lax
from jax.experimental import pallas as pl
from jax.experimental.pallas import tpu as pltpu

F32, BF16, I32 = jnp.float32, jnp.bfloat16, jnp.int32

D_MODEL = 1024
N_HEADS = 8
HEAD_DIM = D_MODEL // N_HEADS
IDX_HEADS = 8
IDX_DIM = 64
TOPK_MAX = 256
CONV_WIDTH = 31
CONV_STATE = CONV_WIDTH - 1
D_FF = 2816
REL_BUCKETS = 32
REL_MAX_DIST = 128
PAGE_SIZE = 128
DEPTH = 2
ALPHA = (2 * DEPTH) ** 0.25
LN_EPS = 1e-5

LANES = 128
SUBLANES = 8
V7X_VMEM_LIMIT_BYTES = 56 * 1024 * 1024

LOG2E = math.log2(math.e)
NEG_INF = float("-inf")
M_FLOOR = -3.0e38
INT_MIN = -(2 ** 31)

BQ = 256
CH = 128
TK = 1024
IDX_GROUP = 4
QK_ROWS = 128
PV_KEYS = 256
BAND = CH + BQ

S1_PAGES = 32
S3_PAGES = 8
SELECT_SEQS = 8


def _bucket_lower_bounds():
    max_exact = REL_BUCKETS // 2
    lows = [None] * REL_BUCKETS
    for d in range(0, REL_MAX_DIST + 1):
        if d < max_exact:
            b = d
        else:
            b = max_exact + int(math.log(d / max_exact) / math.log(REL_MAX_DIST / max_exact)
                                * (REL_BUCKETS - max_exact))
            b = min(b, REL_BUCKETS - 1)
        if lows[b] is None:
            lows[b] = d
    nxt = REL_MAX_DIST
    for b in range(REL_BUCKETS - 1, -1, -1):
        if lows[b] is None:
            lows[b] = nxt
        nxt = lows[b]
    return lows


BUCKET_LOW = _bucket_lower_bounds()


def _mm(a, b):
    return jnp.dot(a, b, preferred_element_type=F32)


def _mm_nt(a, b):
    return lax.dot_general(a, b, (((1,), (1,)), ((), ())), preferred_element_type=F32)


def _layer_norm(y, g, b):
    mu = jnp.mean(y, axis=-1, keepdims=True)
    d = y - mu
    var = jnp.mean(d * d, axis=-1, keepdims=True)
    return d * lax.rsqrt(var + LN_EPS) * g + b


def _params(n_axes):
    return pltpu.CompilerParams(dimension_semantics=("arbitrary",) * n_axes,
                                vmem_limit_bytes=V7X_VMEM_LIMIT_BYTES)


def _resident(shape):
    nd = len(shape)
    return pl.BlockSpec(shape, lambda *_: (0,) * nd, pipeline_mode=pl.Buffered(1))


def _row_tile(m):
    for t in (512, 256, 128, 64, 32, 16, 8):
        if m % t == 0:
            return t
    raise ValueError(f"row count {m} is not a multiple of 8")


def _bias_from_dist(dist, rb_ref, h):
    bias = jnp.full(dist.shape, rb_ref[0, h] * LOG2E, F32)
    for b in range(1, REL_BUCKETS):
        bias = jnp.where(dist >= BUCKET_LOW[b], rb_ref[b, h] * LOG2E, bias)
    return bias


def _key_to_f32(k):
    bits = jnp.where(k < 0, k ^ jnp.int32(0x7FFFFFFF), k)
    return lax.bitcast_convert_type(bits, F32)


def _key_hi16(x):
    bits = lax.bitcast_convert_type(x, I32)
    key = bits ^ ((bits >> 31) & jnp.int32(0x7FFFFFFF))
    return (key >> 16).astype(jnp.int16)


def _key_lo16(x):
    bits = lax.bitcast_convert_type(x, I32)
    key = bits ^ ((bits >> 31) & jnp.int32(0x7FFFFFFF))
    return ((key & jnp.int32(0xFFFF)) + I16_MIN).astype(jnp.int16)


I16_MIN = -(2 ** 15)
PACKED_ROWS = 2 * SUBLANES
KEY_NEG_INF = (0xFF800000 ^ 0x7FFFFFFF) - (1 << 32)
NEG_INF_HI16 = KEY_NEG_INF >> 16


def _glu_kernel(x_ref, w_ref, b_ref, u_ref):
    d = u_ref.shape[-1]
    xb = x_ref[...].astype(BF16)
    a = _mm(xb, w_ref[:, :d]) + b_ref[:, :d]
    g = _mm(xb, w_ref[:, d:]) + b_ref[:, d:]
    u_ref[...] = a * jax.nn.sigmoid(g)


def glu(x, w_bf, b):
    m, d = x.shape
    tm = _row_tile(m)
    return pl.pallas_call(
        _glu_kernel,
        grid=(m // tm,),
        in_specs=[pl.BlockSpec((tm, d), lambda i: (i, 0)), _resident((d, 2 * d)), _resident((1, 2 * d))],
        out_specs=pl.BlockSpec((tm, d), lambda i: (i, 0)),
        out_shape=jax.ShapeDtypeStruct((m, d), F32),
        compiler_params=_params(1),
        name="glu",
    )(x, w_bf, b.reshape(1, 2 * d))


CONV_HALO = 32
CONV_ROWS = 64


def _conv_prompt_kernel(u_ref, halo_ref, wdw_ref, bdw_ref, g_ref, b_ref, z_ref, win_ref, y_ref):
    i = pl.program_id(0)
    tm = u_ref.shape[0]
    ncb = win_ref.shape[0]
    halo = jnp.where(i > 0, halo_ref[...], 0.0)
    for cb in range(ncb):
        win_ref[cb, 0:CONV_HALO, :] = halo[:, cb * LANES:(cb + 1) * LANES]
        win_ref[cb, CONV_HALO:CONV_HALO + tm, :] = u_ref[:, cb * LANES:(cb + 1) * LANES]
    shift = CONV_HALO - CONV_STATE

    def cb_body(cb, carry):
        w = wdw_ref[cb]
        for r in range(tm // CONV_ROWS):
            acc = jnp.zeros((CONV_ROWS, LANES), F32)
            for j in range(CONV_WIDTH):
                acc = acc + win_ref[cb, pl.ds(r * CONV_ROWS + shift + j, CONV_ROWS), :] * w[j:j + 1, :]
            y_ref[cb, r * CONV_ROWS:(r + 1) * CONV_ROWS, :] = acc
        return carry

    lax.fori_loop(0, ncb, cb_body, 0)
    y = jnp.concatenate([y_ref[cb] for cb in range(ncb)], axis=1) + bdw_ref[...]
    z = _layer_norm(y, g_ref[...], b_ref[...])
    z_ref[...] = (z * jax.nn.sigmoid(z)).astype(BF16)


def conv_prompt(u, w_dw, b_dw, g_n, b_n):
    m, d = u.shape
    tm = _row_tile(m)
    ncb = d // LANES
    wdw = jnp.pad(w_dw, ((0, 1), (0, 0))).reshape(CONV_WIDTH + 1, ncb, LANES).transpose(1, 0, 2)
    hb = tm // CONV_HALO
    return pl.pallas_call(
        _conv_prompt_kernel,
        grid=(m // tm,),
        in_specs=[pl.BlockSpec((tm, d), lambda i: (i, 0)),
                  pl.BlockSpec((CONV_HALO, d), lambda i: (jnp.maximum(i * hb - 1, 0), 0)),
                  _resident((ncb, CONV_WIDTH + 1, LANES)),
                  _resident((1, d)), _resident((1, d)), _resident((1, d))],
        out_specs=pl.BlockSpec((tm, d), lambda i: (i, 0)),
        out_shape=jax.ShapeDtypeStruct((m, d), BF16),
        scratch_shapes=[pltpu.VMEM((ncb, tm + CONV_HALO, LANES), F32), pltpu.VMEM((ncb, tm, LANES), F32)],
        compiler_params=_params(1),
        name="conv_prompt",
    )(u, u, wdw, b_dw.reshape(1, d), g_n.reshape(1, d), b_n.reshape(1, d))


def _conv_sample_kernel(st_ref, u_ref, wdw_ref, bdw_ref, g_ref, b_ref, z_ref, ns_ref, win_ref):
    t = u_ref.shape[0]
    win_ref[0:CONV_STATE, :] = st_ref[...]
    win_ref[CONV_STATE:CONV_STATE + t, :] = u_ref[...]
    acc = jnp.zeros(u_ref.shape, F32)
    for j in range(CONV_WIDTH):
        acc = acc + win_ref[j:j + t, :] * wdw_ref[j:j + 1, :]
    z = _layer_norm(acc + bdw_ref[...], g_ref[...], b_ref[...])
    z_ref[...] = z * jax.nn.sigmoid(z)
    ns_ref[...] = win_ref[t:t + CONV_STATE, :]


def conv_sample(u, state, w_dw, b_dw, g_n, b_n):
    b, t, d = u.shape
    return pl.pallas_call(
        _conv_sample_kernel,
        grid=(b,),
        in_specs=[pl.BlockSpec((None, CONV_STATE, d), lambda i: (i, 0, 0)),
                  pl.BlockSpec((None, t, d), lambda i: (i, 0, 0)),
                  _resident((CONV_WIDTH + 1, d)), _resident((1, d)), _resident((1, d)), _resident((1, d))],
        out_specs=[pl.BlockSpec((None, t, d), lambda i: (i, 0, 0)),
                   pl.BlockSpec((None, CONV_STATE, d), lambda i: (i, 0, 0))],
        out_shape=[jax.ShapeDtypeStruct((b, t, d), F32), jax.ShapeDtypeStruct((b, CONV_STATE, d), F32)],
        scratch_shapes=[pltpu.VMEM((CONV_STATE + t + 2, d), F32)],
        compiler_params=_params(1),
        name="conv_sample",
    )(state, u, jnp.pad(w_dw, ((0, 1), (0, 0))), b_dw.reshape(1, d), g_n.reshape(1, d), b_n.reshape(1, d))


def _proj_res_ln_kernel(z_ref, w_ref, bias_ref, x_ref, g_ref, b_ref, o_ref):
    m = _mm(z_ref[...].astype(BF16), w_ref[...]) + bias_ref[...]
    o_ref[...] = _layer_norm(ALPHA * x_ref[...] + m, g_ref[...], b_ref[...])


def proj_res_ln(z, w_bf, bias, x, g, b):
    m, d = x.shape
    k = z.shape[1]
    tm = _row_tile(m)
    return pl.pallas_call(
        _proj_res_ln_kernel,
        grid=(m // tm,),
        in_specs=[pl.BlockSpec((tm, k), lambda i: (i, 0)), _resident((k, d)), _resident((1, d)),
                  pl.BlockSpec((tm, d), lambda i: (i, 0)), _resident((1, d)), _resident((1, d))],
        out_specs=pl.BlockSpec((tm, d), lambda i: (i, 0)),
        out_shape=jax.ShapeDtypeStruct((m, d), F32),
        compiler_params=_params(1),
        name="proj_res_ln",
    )(z, w_bf, bias.reshape(1, d), x, g.reshape(1, d), b.reshape(1, d))


FFN_CHUNK = 256


def _ffn_kernel(x_ref, win_ref, wout_ref, g_ref, b_ref, o_ref):
    x = x_ref[...]
    xb = x.astype(BF16)
    dff = wout_ref.shape[0]
    acc = jnp.zeros(x.shape, F32)
    for c in range(dff // FFN_CHUNK):
        lo = c * FFN_CHUNK
        gate = _mm(xb, win_ref[:, lo:lo + FFN_CHUNK])
        up = _mm(xb, win_ref[:, dff + lo:dff + lo + FFN_CHUNK])
        act = (gate * jax.nn.sigmoid(gate) * up).astype(BF16)
        acc = acc + _mm(act, wout_ref[lo:lo + FFN_CHUNK, :])
    o_ref[...] = _layer_norm(ALPHA * x + acc, g_ref[...], b_ref[...])


def ffn(x, win_bf, wout_bf, g, b):
    m, d = x.shape
    dff = wout_bf.shape[0]
    tm = _row_tile(m)
    return pl.pallas_call(
        _ffn_kernel,
        grid=(m // tm,),
        in_specs=[pl.BlockSpec((tm, d), lambda i: (i, 0)), _resident((d, 2 * dff)), _resident((dff, d)),
                  _resident((1, d)), _resident((1, d))],
        out_specs=pl.BlockSpec((tm, d), lambda i: (i, 0)),
        out_shape=jax.ShapeDtypeStruct((m, d), F32),
        compiler_params=_params(1),
        name="ffn",
    )(x, win_bf, wout_bf, g.reshape(1, d), b.reshape(1, d))


Q_SCALE = HEAD_DIM ** -0.5 * LOG2E
W_SCALE = IDX_HEADS ** -0.5 * IDX_DIM ** -0.5
QI_PAD = IDX_HEADS * LANES
V_ROWS = HEAD_DIM + 16
VT_ROWS = N_HEADS * V_ROWS


def _split_attn_weights(w_in):
    d = D_MODEL
    o3 = 3 * d
    o4 = o3 + IDX_HEADS * IDX_DIM
    o5 = o4 + IDX_DIM
    wq, wk, wv = w_in[:, :d], w_in[:, d:2 * d], w_in[:, 2 * d:o3]
    wqi = w_in[:, o3:o4].reshape(d, IDX_HEADS, IDX_DIM)
    wqi = jnp.pad(wqi, ((0, 0), (0, 0), (0, LANES - IDX_DIM))).reshape(d, QI_PAD)
    wkw = jnp.pad(w_in[:, o4:], ((0, 0), (0, LANES - IDX_DIM - IDX_HEADS)))
    return tuple(a.astype(BF16) for a in (wq, wk, wv, wqi, wkw))


def _attn_proj_t_kernel(x_ref, wk_ref, wv_ref, wkw_ref, wqT_ref, wvT_ref, wqiT_ref, wwT_ref,
                        k32_ref, v32_ref, kiw_ref, qT_ref, kbf_ref, vT_ref, qiT_ref, kip_ref, wT_ref):
    xb = x_ref[...].astype(BF16)
    k = _mm(xb, wk_ref[...])
    k32_ref[...] = k
    kbf_ref[...] = k.astype(BF16)
    v32_ref[...] = _mm(xb, wv_ref[...])
    kiw = _mm(xb, wkw_ref[...])
    kiw_ref[...] = kiw
    lane = lax.broadcasted_iota(I32, kiw.shape, 1)
    kip_ref[...] = jnp.where(lane < IDX_DIM, kiw, 0.0).astype(BF16)
    qT_ref[...] = (_mm_nt(wqT_ref[...], xb) * Q_SCALE).astype(BF16)
    vt = _mm_nt(wvT_ref[...], xb).astype(BF16)
    ones = jnp.ones((V_ROWS - HEAD_DIM, vt.shape[1]), BF16)
    for h in range(N_HEADS):
        vT_ref[h * V_ROWS:h * V_ROWS + HEAD_DIM, :] = vt[h * HEAD_DIM:(h + 1) * HEAD_DIM]
        vT_ref[h * V_ROWS + HEAD_DIM:(h + 1) * V_ROWS, :] = ones
    qiT_ref[...] = _mm_nt(wqiT_ref[...], xb).astype(BF16)
    wT_ref[...] = _mm_nt(wwT_ref[...], xb) * W_SCALE


def attn_proj_t(x, w_in):
    m, d = x.shape
    tm = _row_tile(m)
    wq, wk, wv, wqi, wkw = _split_attn_weights(w_in)
    o5 = 3 * d + IDX_HEADS * IDX_DIM + IDX_DIM
    wwT = jnp.pad(w_in[:, o5:].T, ((0, 16 - IDX_HEADS), (0, 0))).astype(BF16)
    row = lambda n: pl.BlockSpec((tm, n), lambda i: (i, 0))
    col = lambda n: pl.BlockSpec((n, tm), lambda i: (0, i))
    return pl.pallas_call(
        _attn_proj_t_kernel,
        grid=(m // tm,),
        in_specs=[row(d), _resident((d, d)), _resident((d, d)), _resident((d, LANES)),
                  _resident((d, d)), _resident((d, d)), _resident((QI_PAD, d)), _resident((16, d))],
        out_specs=[row(d), row(d), row(LANES), col(d), row(d), col(VT_ROWS), col(QI_PAD), row(LANES), col(16)],
        out_shape=[jax.ShapeDtypeStruct((m, d), F32), jax.ShapeDtypeStruct((m, d), F32),
                   jax.ShapeDtypeStruct((m, LANES), F32), jax.ShapeDtypeStruct((d, m), BF16),
                   jax.ShapeDtypeStruct((m, d), BF16), jax.ShapeDtypeStruct((VT_ROWS, m), BF16),
                   jax.ShapeDtypeStruct((QI_PAD, m), BF16), jax.ShapeDtypeStruct((m, LANES), BF16),
                   jax.ShapeDtypeStruct((16, m), F32)],
        compiler_params=_params(1),
        name="attn_proj_t",
    )(x, wk, wv, wkw, wq.T, wv.T, wqi.T, wwT)


def _attn_proj_s_kernel(x_ref, wq_ref, wk_ref, wv_ref, wqi_ref, wkw_ref,
                        q_ref, k32_ref, v32_ref, qi_ref, kiw_ref):
    xb = x_ref[...].astype(BF16)
    q_ref[...] = (_mm(xb, wq_ref[...]) * Q_SCALE).astype(BF16)
    k32_ref[...] = _mm(xb, wk_ref[...])
    v32_ref[...] = _mm(xb, wv_ref[...])
    qi_ref[...] = _mm(xb, wqi_ref[...]).astype(BF16)
    kiw_ref[...] = _mm(xb, wkw_ref[...])


def attn_proj_s(x, w_in):
    m, d = x.shape
    tm = _row_tile(m)
    wq, wk, wv, wqi, wkw = _split_attn_weights(w_in)
    row = lambda n: pl.BlockSpec((tm, n), lambda i: (i, 0))
    return pl.pallas_call(
        _attn_proj_s_kernel,
        grid=(m // tm,),
        in_specs=[row(d), _resident((d, d)), _resident((d, d)), _resident((d, d)),
                  _resident((d, QI_PAD)), _resident((d, LANES))],
        out_specs=[row(d), row(d), row(d), row(QI_PAD), row(LANES)],
        out_shape=[jax.ShapeDtypeStruct((m, d), BF16), jax.ShapeDtypeStruct((m, d), F32),
                   jax.ShapeDtypeStruct((m, d), F32), jax.ShapeDtypeStruct((m, QI_PAD), BF16),
                   jax.ShapeDtypeStruct((m, LANES), F32)],
        compiler_params=_params(1),
        name="attn_proj_s",
    )(x, wq, wk, wv, wqi, wkw)


def _prompt_schedule(t):
    qb_of, kt_of, fl = [], [], []
    for qb in range(t // BQ):
        nb = -(-max(qb * BQ - CH, 0) // TK)
        qb_of.append(qb); kt_of.append(0); fl.append(1 | (2 if nb == 0 else 0))
        for kt in range(nb):
            qb_of.append(qb); kt_of.append(kt); fl.append(2 if kt == nb - 1 else 0)
    return (np.asarray(qb_of, np.int32), np.asarray(kt_of, np.int32), np.asarray(fl, np.int32))


def _attn_prompt_kernel(qb_ref, kt_ref, fl_ref,
                        qT_ref, qiT_ref, wT_ref, kip_ref, kb_ref, vTb_ref, kp_ref, kd_ref, vTp_ref, vTd_ref,
                        rb_ref, o_ref,
                        s_ref, h_ref, thr_ref, bb_ref, m_ref, l_ref, acc_ref, madd_ref, *, k_sel, idx_bits):
    step = pl.program_id(0)
    qb = qb_ref[step]
    kt = kt_ref[step]
    flags = fl_ref[step]
    is_band = (flags & 1) != 0
    is_last = (flags & 2) != 0
    q0 = qb * BQ
    nch = 2 * (qb + 1)
    cband = 2 * qb - 1
    kf = float(k_sel)

    @pl.when(step == 0)
    def _build_band_bias():
        r = lax.broadcasted_iota(I32, (BAND, BQ), 0)
        j = lax.broadcasted_iota(I32, (BAND, BQ), 1)
        dist = CH + j - r
        for h in range(N_HEADS):
            bb_ref[h] = _bias_from_dist(dist, rb_ref, h)

    def head_cols(h):
        return slice(h * HEAD_DIM, (h + 1) * HEAD_DIM)

    def head_vrows(h):
        return slice(h * V_ROWS, (h + 1) * V_ROWS)

    def qk(h, kh):
        n = kh.shape[0] // QK_ROWS
        return jnp.concatenate([_mm(kh[i * QK_ROWS:(i + 1) * QK_ROWS], qT_ref[head_cols(h), :])
                                for i in range(n)], axis=0)

    def softmax_pv(h, lt, vth, madd, bias, cbias):
        hs = head_cols(h)
        if bias is not None:
            lt = lt + bias
        lt = lt + madd
        mx = jnp.max(lt, axis=0, keepdims=True) + cbias
        m_old = m_ref[h:h + 1, :]
        m_new = jnp.maximum(m_old, mx)
        alpha = jnp.exp2(m_old - m_new)
        pb = jnp.exp2(lt - (m_new - cbias)).astype(BF16)
        r = pb.shape[0]
        if r % (2 * PV_KEYS) == 0:
            pv = _mm(vth[:, :r // 2], pb[:r // 2]) + _mm(vth[:, r // 2:], pb[r // 2:])
        else:
            pv = _mm(vth, pb)
        acc_ref[hs, :] = alpha * acc_ref[hs, :] + pv[0:HEAD_DIM]
        l_ref[h:h + 1, :] = alpha * l_ref[h:h + 1, :] + pv[HEAD_DIM:HEAD_DIM + 1]
        m_ref[h:h + 1, :] = m_new

    def attend_heads(keys_of, vals_of, madd, bias_of, cbias_of):
        lt = qk(0, keys_of(0))
        for h in range(N_HEADS):
            lt_next = qk(h + 1, keys_of(h + 1)) if h + 1 < N_HEADS else None
            softmax_pv(h, lt, vals_of(h), madd, bias_of(h), cbias_of(h))
            lt = lt_next

    @pl.when(is_band)
    def _band_step():
        def idx_body(gi, carry):
            rows = IDX_GROUP * CH
            kic = kip_ref[pl.ds(pl.multiple_of(gi * rows, rows), rows), :]
            acc = None
            for h in range(IDX_HEADS):
                sc = _mm(kic, qiT_ref[h * LANES:(h + 1) * LANES, :])
                val = jnp.maximum(sc, 0.0) * wT_ref[h:h + 1, :]
                acc = val if acc is None else acc + val
            acc = jnp.where(acc == 0.0, 0.0, acc)
            hi = _key_hi16(acc)
            for i in range(IDX_GROUP):
                s_ref[gi * IDX_GROUP + i] = acc[i * CH:(i + 1) * CH]
                h_ref[gi * IDX_GROUP + i] = hi[i * CH:(i + 1) * CH]
            return carry

        ngrp = (nch + IDX_GROUP - 1) // IDX_GROUP
        lax.fori_loop(0, ngrp, idx_body, 0)
        rr = lax.broadcasted_iota(I32, (CH, BQ), 0)
        jj = lax.broadcasted_iota(I32, (CH, BQ), 1)
        for i in range(BQ // CH):
            c = 2 * qb + i
            masked = jnp.where(i * CH + rr <= jj, s_ref[c], NEG_INF)
            s_ref[c] = masked
            h_ref[c] = _key_hi16(masked)

        @pl.when(ngrp * IDX_GROUP > nch)
        def _mask_group_tail():
            for c in range(IDX_GROUP - BQ // CH):
                s_ref[nch + c] = jnp.full((CH, BQ), NEG_INF, F32)
                h_ref[nch + c] = jnp.full((CH, BQ), NEG_INF_HI16, jnp.int16)

        def count(pred):
            def body(g, acc):
                for c in range(IDX_GROUP):
                    ci = g * IDX_GROUP + c
                    ind = jnp.where(pred(ci, s_ref[ci]), 1.0, 0.0)
                    for r in range(CH // SUBLANES):
                        acc = acc + ind[r * SUBLANES:(r + 1) * SUBLANES]
                return acc
            acc = lax.fori_loop(0, ngrp, body, jnp.zeros((SUBLANES, BQ), F32))
            return jnp.broadcast_to(jnp.sum(acc, axis=0, keepdims=True), (SUBLANES, BQ))

        def count_hi(pred):
            one, zero = jnp.ones((), jnp.int16), jnp.zeros((), jnp.int16)

            def body(g, acc):
                for c in range(IDX_GROUP):
                    ind = jnp.where(pred(h_ref[g * IDX_GROUP + c]), one, zero)
                    for r in range(CH // PACKED_ROWS):
                        acc = acc + ind[r * PACKED_ROWS:(r + 1) * PACKED_ROWS]
                return acc
            acc = lax.fori_loop(0, ngrp, body, jnp.zeros((PACKED_ROWS, BQ), jnp.int16))
            tot = jnp.sum(acc.astype(I32).astype(F32), axis=0, keepdims=True)
            return jnp.broadcast_to(tot, (SUBLANES, BQ))

        def row16(v):
            return jnp.broadcast_to(v[0:1, :], (PACKED_ROWS, BQ)).astype(jnp.int16)[0:1, :]

        def search16(cnt_min, target):
            c0 = count_hi(lambda x: x >= jnp.zeros((), jnp.int16))
            ok0 = jnp.logical_and(c0 >= target, cnt_min != target)
            ans = jnp.where(ok0, 0, I16_MIN).astype(I32)
            cnt = jnp.where(ok0, c0, cnt_min)

            def cond(carry):
                i, _, cnt = carry
                return jnp.logical_and(i < 15, jnp.max(jnp.abs(cnt - target)) > 0.5)

            def body(carry):
                i, ans, cnt = carry
                cand = ans + jnp.left_shift(jnp.int32(1), 14 - i)
                c16 = row16(cand)
                c = count_hi(lambda x: x >= c16)
                ok = jnp.logical_and(c >= target, cnt != target)
                return i + 1, jnp.where(ok, cand, ans), jnp.where(ok, c, cnt)

            _, ans, cnt = lax.while_loop(cond, body, (jnp.int32(0), ans, cnt))
            return ans, cnt

        total = (ngrp * (IDX_GROUP * CH)).astype(F32)
        k_vec = jnp.full((SUBLANES, BQ), kf, F32)
        ans_hi, cnt_hi = search16(jnp.broadcast_to(total, (SUBLANES, BQ)), k_vec)
        p16 = row16(ans_hi)
        n_above = count_hi(lambda x: x > p16)
        def lo_body(g, carry):
            for c in range(IDX_GROUP):
                ci = g * IDX_GROUP + c
                h_ref[ci] = jnp.where(h_ref[ci] == p16, _key_lo16(s_ref[ci]), jnp.int16(I16_MIN))
            return carry

        lax.fori_loop(0, ngrp, lo_body, 0)
        ans_lo, cnt_lo = search16(cnt_hi - n_above, k_vec - n_above)
        ans = ans_hi * (2 ** 16) + (ans_lo - I16_MIN)
        cnt = n_above + cnt_lo
        none = ans <= KEY_NEG_INF
        thr = jnp.where(none, NEG_INF, _key_to_f32(ans))
        thr_ref[...] = thr

        has = jnp.logical_and(cnt > kf, jnp.logical_not(none))
        any_ties = jnp.max(jnp.where(has, 1.0, 0.0)) > 0.5

        @pl.when(any_ties)
        def _resolve_ties():
            t1 = thr[0:1, :]
            n_gt = count(lambda c, x: x > t1)
            need = jnp.where(has, kf - n_gt, 3.0e38)
            rows = lax.broadcasted_iota(I32, (CH, BQ), 0)

            def j_body(i, j):
                cj = j + jnp.left_shift(jnp.int32(1), idx_bits - 1 - i)
                c1 = cj[0:1, :]
                f = count(lambda c, x: jnp.logical_and(x == t1, c * CH + rows < c1))
                return jnp.where(f < need, cj, j)

            jstar = lax.fori_loop(0, idx_bits, j_body, jnp.zeros((SUBLANES, BQ), I32))
            j1 = jstar[0:1, :]
            h1 = has[0:1, :]

            def demote(c, carry):
                x = s_ref[c]
                drop = jnp.logical_and(jnp.logical_and(x == t1, c * CH + rows > j1), h1)
                s_ref[c] = jnp.where(drop, NEG_INF, x)
                return carry

            lax.fori_loop(0, nch, demote, 0)

        m_ref[...] = jnp.full(m_ref.shape, M_FLOOR, F32)
        l_ref[...] = jnp.zeros(l_ref.shape, F32)
        acc_ref[...] = jnp.zeros(acc_ref.shape, F32)
        thr1 = thr_ref[0:1, :]
        cprev = jnp.maximum(cband, 0)
        sc = jnp.concatenate([s_ref[cprev], s_ref[2 * qb], s_ref[2 * qb + 1]], axis=0)
        r = lax.broadcasted_iota(I32, (BAND, BQ), 0)
        j = lax.broadcasted_iota(I32, (BAND, BQ), 1)
        ok = jnp.logical_and(sc >= thr1, r - CH <= j)
        ok = jnp.logical_and(ok, jnp.logical_or(r >= CH, qb > 0))
        madd = jnp.where(ok, 0.0, NEG_INF)
        attend_heads(lambda h: jnp.concatenate([kp_ref[:, head_cols(h)], kd_ref[:, head_cols(h)]], axis=0),
                     lambda h: jnp.concatenate([vTp_ref[head_vrows(h), :], vTd_ref[head_vrows(h), :]], axis=1),
                     madd, lambda h: bb_ref[h], lambda h: 0.0)

    @pl.when(jnp.logical_not(is_band))
    def _bulk_step():
        thr1 = thr_ref[0:1, :]
        for i in range(TK // CH):
            c = kt * (TK // CH) + i
            x = s_ref[jnp.minimum(c, cband - 1)]
            sel = jnp.where(x >= thr1, 0.0, NEG_INF)
            madd_ref[i * CH:(i + 1) * CH, :] = jnp.where(c < cband, sel, NEG_INF)
        attend_heads(lambda h: kb_ref[:, head_cols(h)], lambda h: vTb_ref[head_vrows(h), :], madd_ref[...],
                     lambda h: None, lambda h: rb_ref[REL_BUCKETS - 1, h] * LOG2E)

    @pl.when(is_last)
    def _finalize():
        for h in range(N_HEADS):
            hs = slice(h * HEAD_DIM, (h + 1) * HEAD_DIM)
            lsum = l_ref[h:h + 1, :]
            o_ref[:, hs] = (acc_ref[hs, :] / lsum).T.astype(BF16)


def attn_prompt(qT, qiT, wT, kip, kbf, vT, rel_bias):
    d, t = qT.shape
    k_sel = min(TOPK_MAX, t // 4)
    qb_of, kt_of, fl = _prompt_schedule(t)
    n_steps = len(qb_of)
    tk = min(TK, t)
    qcol = lambda n: pl.BlockSpec((n, BQ), lambda s, qb, kt, fl: (0, qb[s]))
    in_specs = [
        qcol(d), qcol(QI_PAD), qcol(16),
        pl.BlockSpec((t, LANES), lambda s, qb, kt, fl: (0, 0), pipeline_mode=pl.Buffered(1)),
        pl.BlockSpec((tk, d), lambda s, qb, kt, fl: (kt[s], 0)),
        pl.BlockSpec((VT_ROWS, tk), lambda s, qb, kt, fl: (0, kt[s])),
        pl.BlockSpec((CH, d), lambda s, qb, kt, fl: (jnp.maximum(2 * qb[s] - 1, 0), 0)),
        pl.BlockSpec((BQ, d), lambda s, qb, kt, fl: (qb[s], 0)),
        pl.BlockSpec((VT_ROWS, CH), lambda s, qb, kt, fl: (0, jnp.maximum(2 * qb[s] - 1, 0))),
        pl.BlockSpec((VT_ROWS, BQ), lambda s, qb, kt, fl: (0, qb[s])),
        pl.BlockSpec(memory_space=pltpu.SMEM),
    ]
    grid_spec = pltpu.PrefetchScalarGridSpec(
        num_scalar_prefetch=3,
        grid=(n_steps,),
        in_specs=in_specs,
        out_specs=pl.BlockSpec((BQ, d), lambda s, qb, kt, fl: (qb[s], 0)),
        scratch_shapes=[
            pltpu.VMEM((t // CH, CH, BQ), F32),
            pltpu.VMEM((t // CH, CH, BQ), jnp.int16),
            pltpu.VMEM((SUBLANES, BQ), F32),
            pltpu.VMEM((N_HEADS, BAND, BQ), F32),
            pltpu.VMEM((N_HEADS, BQ), F32),
            pltpu.VMEM((N_HEADS, BQ), F32),
            pltpu.VMEM((d, BQ), F32),
            pltpu.VMEM((tk, BQ), F32),
        ],
    )
    kern = functools.partial(_attn_prompt_kernel, k_sel=k_sel, idx_bits=max(t.bit_length() - 1, 1))
    return pl.pallas_call(
        kern,
        grid_spec=grid_spec,
        out_shape=jax.ShapeDtypeStruct((t, d), BF16),
        compiler_params=_params(1),
        name="attn_prompt",
    )(jnp.asarray(qb_of), jnp.asarray(kt_of), jnp.asarray(fl),
      qT, qiT, wT, kip, kbf, vT, kbf, kbf, vT, vT, rel_bias)


def _sample_scores_kernel(pt_ref, qi_ref, w_ref, *rest, n_groups):
    pages = rest[:S1_PAGES]
    kin_ref, o_ref, kbuf_ref = rest[S1_PAGES:]
    b = pl.program_id(0)
    g = pl.program_id(1)
    t = o_ref.shape[1]

    @pl.when(jnp.logical_and(b == 0, g == 0))
    def _zero_pad_rows():
        kbuf_ref[...] = jnp.zeros(kbuf_ref.shape, BF16)

    def head_sum(val):
        s = val[0:t]
        for h in range(1, IDX_HEADS):
            s = s + val[h * t:(h + 1) * t]
        return s

    wcol = w_ref[:, 0:1]

    @pl.when(g < n_groups)
    def _past_pages():
        for i in range(S1_PAGES):
            kbuf_ref[0:IDX_DIM, i * PAGE_SIZE:(i + 1) * PAGE_SIZE] = pages[i][...].astype(BF16)
        s = head_sum(jnp.maximum(_mm(qi_ref[...], kbuf_ref[...]), 0.0) * wcol)
        for i in range(S1_PAGES):
            o_ref[i] = s[:, i * PAGE_SIZE:(i + 1) * PAGE_SIZE]

    @pl.when(g == n_groups)
    def _new_keys():
        s = head_sum(jnp.maximum(_mm_nt(qi_ref[...], kin_ref[...]), 0.0) * wcol)
        row = lax.broadcasted_iota(I32, s.shape, 0)
        lane = lax.broadcasted_iota(I32, s.shape, 1)
        o_ref[0] = jnp.where(lane <= row, s, NEG_INF)
        for i in range(1, S1_PAGES):
            o_ref[i] = jnp.full(s.shape, NEG_INF, F32)


def sample_scores(page_table, qi_ht, w_ht, cache_kidx, layer, ki_new):
    b, n_pages = page_table.shape
    t = qi_ht.shape[1] // IDX_HEADS
    n_groups = n_pages // S1_PAGES
    page_spec = lambda i: pl.BlockSpec(
        (None, None, IDX_DIM, PAGE_SIZE),
        lambda bi, g, pt: (layer, pt[bi, jnp.minimum(g, n_groups - 1) * S1_PAGES + i], 0, 0))
    grid_spec = pltpu.PrefetchScalarGridSpec(
        num_scalar_prefetch=1,
        grid=(b, n_groups + 1),
        in_specs=[pl.BlockSpec((None, IDX_HEADS * t, LANES), lambda bi, g, pt: (bi, 0, 0)),
                  pl.BlockSpec((None, IDX_HEADS * t, LANES), lambda bi, g, pt: (bi, 0, 0))]
                 + [page_spec(i) for i in range(S1_PAGES)]
                 + [pl.BlockSpec((None, PAGE_SIZE, LANES), lambda bi, g, pt: (bi, 0, 0))],
        out_specs=pl.BlockSpec((None, S1_PAGES, t, LANES), lambda bi, g, pt: (bi, g, 0, 0)),
        scratch_shapes=[pltpu.VMEM((LANES, S1_PAGES * PAGE_SIZE), BF16)],
    )
    kidx_t = jnp.swapaxes(cache_kidx, 2, 3)
    return pl.pallas_call(
        functools.partial(_sample_scores_kernel, n_groups=n_groups),
        grid_spec=grid_spec,
        out_shape=jax.ShapeDtypeStruct((b, (n_groups + 1) * S1_PAGES, t, LANES), F32),
        compiler_params=_params(2),
        name="sample_scores",
    )(page_table, qi_ht, w_ht, *([kidx_t] * S1_PAGES), ki_new)


def _sample_select_kernel(s_ref, o_ref, *, k_sel, past, idx_bits):
    x = s_ref[...]
    nb, nck, t, _ = x.shape
    kf = float(k_sel)
    cidx = lax.broadcasted_iota(I32, x.shape, 1)
    row = lax.broadcasted_iota(I32, x.shape, 2)
    lane = lax.broadcasted_iota(I32, x.shape, 3)
    idx = cidx * LANES + lane

    def count(pred):
        per_lane = jnp.sum(jnp.where(pred, 1.0, 0.0), axis=1)
        return jnp.broadcast_to(jnp.sum(per_lane, axis=2, keepdims=True), (nb, t, LANES))

    def count_ge(cand_key):
        return count(x >= _key_to_f32(cand_key)[:, None])

    total = float(nck * LANES)
    c0 = count(x >= 0.0)
    ans = jnp.where(c0 >= kf, 0, INT_MIN).astype(I32)
    cnt = jnp.where(c0 >= kf, c0, total)

    def bit_cond(carry):
        i, _, cnt = carry
        return jnp.logical_and(i < 31, jnp.max(jnp.abs(cnt - kf)) > 0.5)

    def bit_body(carry):
        i, ans, cnt = carry
        cand = ans + jnp.left_shift(jnp.int32(1), 30 - i)
        c = count_ge(cand)
        ok = jnp.logical_and(c >= kf, cnt != kf)
        return i + 1, jnp.where(ok, cand, ans), jnp.where(ok, c, cnt)

    _, ans, cnt = lax.while_loop(bit_cond, bit_body, (jnp.int32(0), ans, cnt))
    none = ans == INT_MIN
    thr = jnp.where(none, NEG_INF, _key_to_f32(ans))
    has = jnp.logical_and(cnt > kf, jnp.logical_not(none))
    any_ties = jnp.max(jnp.where(has, 1.0, 0.0)) > 0.5
    new = idx - past
    valid = jnp.logical_or(idx < past, jnp.logical_and(new < t, new <= row))
    keep = jnp.logical_and(x >= thr[:, None], valid)

    @pl.when(jnp.logical_not(any_ties))
    def _no_ties():
        o_ref[...] = jnp.where(keep, 0.0, NEG_INF)

    @pl.when(any_ties)
    def _resolve_ties():
        n_gt = count(x > thr[:, None])
        need = jnp.where(has, kf - n_gt, 3.0e38)
        eq = x == thr[:, None]

        def j_body(i, j):
            cj = j + jnp.left_shift(jnp.int32(1), idx_bits - 1 - i)
            f = count(jnp.logical_and(eq, idx < cj[:, None]))
            return jnp.where(f < need, cj, j)

        jstar = lax.fori_loop(0, idx_bits, j_body, jnp.zeros((nb, t, LANES), I32))
        drop = jnp.logical_and(jnp.logical_and(eq, idx > jstar[:, None]), has[:, None])
        o_ref[...] = jnp.where(jnp.logical_and(keep, jnp.logical_not(drop)), 0.0, NEG_INF)


def sample_select(scores, past, k_sel):
    b, nck, t, _ = scores.shape
    nb = math.gcd(b, SELECT_SEQS)
    spec = pl.BlockSpec((nb, nck, t, LANES), lambda i: (i, 0, 0, 0))
    kern = functools.partial(_sample_select_kernel, k_sel=k_sel, past=past,
                             idx_bits=(nck * LANES - 1).bit_length())
    return pl.pallas_call(
        kern, grid=(b // nb,), in_specs=[spec], out_specs=spec,
        out_shape=jax.ShapeDtypeStruct(scores.shape, F32),
        compiler_params=_params(1),
        name="sample_select",
    )(scores)


def _sample_attn_kernel(pt_ref, q_ref, madd_ref, *rest, n_groups, t):
    kpages = rest[:S3_PAGES]
    vpages = rest[S3_PAGES:2 * S3_PAGES]
    (kn_ref, vn_ref, rb_ref, o_ref,
     kcat_ref, vcat_ref, m_ref, l_ref, acc_ref, bconst_ref, blast_ref, bnew_ref) = rest[2 * S3_PAGES:]
    b = pl.program_id(0)
    g = pl.program_id(1)
    rows = N_HEADS * t

    @pl.when(jnp.logical_and(b == 0, g == 0))
    def _build_bias_tables():
        tq = lax.broadcasted_iota(I32, (t, LANES), 0)
        ln = lax.broadcasted_iota(I32, (t, LANES), 1)
        for h in range(N_HEADS):
            hs = slice(h * t, (h + 1) * t)
            bconst_ref[hs, :] = jnp.full((t, LANES), rb_ref[REL_BUCKETS - 1, h] * LOG2E, F32)
            blast_ref[hs, :] = _bias_from_dist(PAGE_SIZE + tq - ln, rb_ref, h)
            bnew_ref[hs, :] = _bias_from_dist(tq - ln, rb_ref, h)

    @pl.when(g == 0)
    def _init():
        m_ref[...] = jnp.full(m_ref.shape, M_FLOOR, F32)
        l_ref[...] = jnp.zeros(l_ref.shape, F32)
        acc_ref[...] = jnp.zeros(acc_ref.shape, F32)

    def tile_heads(m8):
        return jnp.concatenate([m8] * N_HEADS, axis=0)

    def attend(kk, vv, bias, madd):
        lg = _mm_nt(q_ref[...], kk) + bias + madd
        mx = jnp.max(lg, axis=1, keepdims=True)
        m_old = m_ref[...]
        m_new = jnp.maximum(m_old, mx)
        alpha = jnp.exp2(m_old - m_new)
        p = jnp.exp2(lg - m_new[:, 0:1])
        l_ref[...] = alpha * l_ref[...] + jnp.sum(p, axis=1, keepdims=True)
        acc_ref[...] = alpha[:, 0:1] * acc_ref[...] + _mm(p.astype(BF16), vv)
        m_ref[...] = m_new

    @pl.when(g < n_groups)
    def _cached_pages():
        for i in range(S3_PAGES):
            rs = slice(i * PAGE_SIZE, (i + 1) * PAGE_SIZE)
            for h in range(N_HEADS):
                hs = slice(h * HEAD_DIM, (h + 1) * HEAD_DIM)
                head_rows = pl.ds(h, PAGE_SIZE, stride=N_HEADS)
                kcat_ref[rs, hs] = kpages[i][head_rows, :].astype(BF16)
                vcat_ref[rs, hs] = vpages[i][head_rows, :].astype(BF16)
        last = jnp.where(g == n_groups - 1, blast_ref[...], bconst_ref[...])
        bias = jnp.concatenate([bconst_ref[...]] * (S3_PAGES - 1) + [last], axis=1)
        madd = jnp.concatenate([tile_heads(madd_ref[i]) for i in range(S3_PAGES)], axis=1)
        attend(kcat_ref[...], vcat_ref[...], bias, madd)

    @pl.when(g == n_groups)
    def _new_keys_and_finish():
        attend(kn_ref[...], vn_ref[...], bnew_ref[...], tile_heads(madd_ref[0]))
        for h in range(N_HEADS):
            hs = slice(h * HEAD_DIM, (h + 1) * HEAD_DIM)
            rs = slice(h * t, (h + 1) * t)
            o_ref[:, hs] = acc_ref[rs, hs] / l_ref[rs, 0:1]


def sample_attn(page_table, q_bd, madd, cache_k, cache_v, layer, k_new, v_new, rel_bias):
    b, n_pages = page_table.shape
    rows, d = q_bd.shape[1:]
    t = rows // N_HEADS
    n_groups = n_pages // S3_PAGES
    page_spec = lambda i: pl.BlockSpec(
        (None, None, PAGE_SIZE * N_HEADS, HEAD_DIM),
        lambda bi, g, pt: (layer, pt[bi, jnp.minimum(g, n_groups - 1) * S3_PAGES + i], 0, 0))
    per_b = lambda shape: pl.BlockSpec((None,) + shape, lambda bi, g, pt: (bi,) + (0,) * len(shape))
    grid_spec = pltpu.PrefetchScalarGridSpec(
        num_scalar_prefetch=1,
        grid=(b, n_groups + 1),
        in_specs=[per_b((rows, d)),
                  pl.BlockSpec((None, S3_PAGES, t, LANES), lambda bi, g, pt: (bi, g, 0, 0))]
                 + [page_spec(i) for i in range(S3_PAGES)] * 2
                 + [per_b((PAGE_SIZE, d)), per_b((PAGE_SIZE, d)), pl.BlockSpec(memory_space=pltpu.SMEM)],
        out_specs=per_b((t, d)),
        scratch_shapes=[pltpu.VMEM((S3_PAGES * PAGE_SIZE, d), BF16), pltpu.VMEM((S3_PAGES * PAGE_SIZE, d), BF16),
                        pltpu.VMEM((rows, LANES), F32), pltpu.VMEM((rows, LANES), F32),
                        pltpu.VMEM((rows, d), F32),
                        pltpu.VMEM((rows, LANES), F32), pltpu.VMEM((rows, LANES), F32),
                        pltpu.VMEM((rows, LANES), F32)],
    )
    return pl.pallas_call(
        functools.partial(_sample_attn_kernel, n_groups=n_groups, t=t),
        grid_spec=grid_spec,
        out_shape=jax.ShapeDtypeStruct((b, t, d), F32),
        compiler_params=_params(2),
        name="sample_attn",
    )(page_table, q_bd, madd, *([cache_k] * S3_PAGES), *([cache_v] * S3_PAGES), k_new, v_new, rel_bias)


def attn_sample(x, cache_k, cache_v, cache_kidx, page_table, layer, w_in, rel_bias):
    b, t, d = x.shape
    past = page_table.shape[1] * PAGE_SIZE
    k_sel = min(TOPK_MAX, (past + t) // 4)
    q, k32, v32, qi, kiw = attn_proj_s(x.reshape(b * t, d), w_in)
    qi_ht = qi.reshape(b, t, IDX_HEADS, LANES).transpose(0, 2, 1, 3).reshape(b, IDX_HEADS * t, LANES)
    wi = kiw[:, IDX_DIM:IDX_DIM + IDX_HEADS] * W_SCALE
    w_ht = jnp.broadcast_to(wi.reshape(b, t, IDX_HEADS).transpose(0, 2, 1).reshape(b, IDX_HEADS * t, 1),
                            (b, IDX_HEADS * t, LANES))
    head_of_col = jnp.arange(d, dtype=I32) // HEAD_DIM
    q_bd = jnp.where(head_of_col[None, None, None, :] == jnp.arange(N_HEADS, dtype=I32)[None, :, None, None],
                     q.reshape(b, 1, t, d), jnp.zeros((), BF16)).reshape(b, N_HEADS * t, d)
    pad_rows = lambda a: jnp.pad(a.reshape(b, t, -1), ((0, 0), (0, PAGE_SIZE - t), (0, 0))).astype(BF16)
    ki_new = pad_rows(jnp.pad(kiw[:, :IDX_DIM], ((0, 0), (0, LANES - IDX_DIM))))
    scores = sample_scores(page_table, qi_ht, w_ht, cache_kidx, layer, ki_new)
    madd = sample_select(scores, past, k_sel)
    page_rows = lambda c: c.reshape(c.shape[0], c.shape[1], PAGE_SIZE * N_HEADS, HEAD_DIM)
    out = sample_attn(page_table, q_bd, madd, page_rows(cache_k), page_rows(cache_v), layer,
                      pad_rows(k32), pad_rows(v32), rel_bias)
    return out.reshape(b * t, d), k32, v32, kiw[:, :IDX_DIM]


def kernel(x_prompt, x_sample, state_conv, cache_k, cache_v, cache_kidx, page_table, rel_bias,
           w_pw1, b_pw1, w_dw, b_dw, conv_norm_g, conv_norm_b, w_pw2, b_pw2,
           w_attn_in, w_attn_out, w_ffn_in, w_ffn_out, ln_mix_g, ln_mix_b, ln_ffn_g, ln_ffn_b):
    bp, tp, d = x_prompt.shape
    bs, ts, _ = x_sample.shape
    assert bp == 1 and d == D_MODEL and tp % TK == 0
    xp = x_prompt.reshape(tp, d)
    xs = x_sample.reshape(bs * ts, d)
    bf = lambda a: a.astype(BF16)

    w1, w2 = bf(w_pw1[0]), bf(w_pw2[0])
    up = glu(xp, w1, b_pw1[0])
    us = glu(xs, w1, b_pw1[0])
    zp = conv_prompt(up, w_dw[0], b_dw[0], conv_norm_g[0], conv_norm_b[0])
    zs, conv_s = conv_sample(us.reshape(bs, ts, d), state_conv[0], w_dw[0], b_dw[0],
                             conv_norm_g[0], conv_norm_b[0])
    conv_p = up[tp - CONV_STATE:]
    xp = proj_res_ln(zp, w2, b_pw2[0], xp, ln_mix_g[0], ln_mix_b[0])
    xs = proj_res_ln(zs.reshape(bs * ts, d), w2, b_pw2[0], xs, ln_mix_g[0], ln_mix_b[0])
    wf_in, wf_out = bf(w_ffn_in[0]), bf(w_ffn_out[0])
    xp = ffn(xp, wf_in, wf_out, ln_ffn_g[0], ln_ffn_b[0])
    xs = ffn(xs, wf_in, wf_out, ln_ffn_g[0], ln_ffn_b[0])

    w_o = bf(w_attn_out[0])
    no_bias = jnp.zeros((d,), F32)
    kp32, vp32, kiwp, qT, kbf, vT, qiT, kip, wT = attn_proj_t(xp, w_attn_in[0])
    ap = attn_prompt(qT, qiT, wT, kip, kbf, vT, rel_bias)
    xp = proj_res_ln(ap, w_o, no_bias, xp, ln_mix_g[1], ln_mix_b[1])
    a_s, ks32, vs32, kis = attn_sample(xs.reshape(bs, ts, d), cache_k, cache_v, cache_kidx, page_table, 0,
                                       w_attn_in[0], rel_bias)
    xs = proj_res_ln(a_s, w_o, no_bias, xs, ln_mix_g[1], ln_mix_b[1])
    wf_in, wf_out = bf(w_ffn_in[1]), bf(w_ffn_out[1])
    xp = ffn(xp, wf_in, wf_out, ln_ffn_g[1], ln_ffn_b[1])
    xs = ffn(xs, wf_in, wf_out, ln_ffn_g[1], ln_ffn_b[1])

    return (xp.reshape(1, tp, d), xs.reshape(bs, ts, d),
            conv_p.reshape(1, 1, CONV_STATE, d), conv_s.reshape(1, bs, CONV_STATE, d),
            kp32.reshape(1, 1, tp, N_HEADS, HEAD_DIM), vp32.reshape(1, 1, tp, N_HEADS, HEAD_DIM),
            kiwp[:, :IDX_DIM].reshape(1, 1, tp, IDX_DIM),
            ks32.reshape(1, bs, ts, N_HEADS, HEAD_DIM), vs32.reshape(1, bs, ts, N_HEADS, HEAD_DIM),
            kis.reshape(1, bs, ts, IDX_DIM))
```

```python
import functools
import math

import numpy as np
import jax
import jax.numpy as jnp
from jax import lax
from jax.experimental import pallas as pl
from jax.experimental.pallas import tpu as pltpu

F32, BF16, I32 = jnp.float32, jnp.bfloat16, jnp.int32

D_MODEL = 1024
N_HEADS = 8
HEAD_DIM = D_MODEL // N_HEADS
IDX_HEADS = 8
IDX_DIM = 64
TOPK_MAX = 256
CONV_WIDTH = 31
CONV_STATE = CONV_WIDTH - 1
D_FF = 2816
REL_BUCKETS = 32
REL_MAX_DIST = 128
PAGE_SIZE = 128
DEPTH = 2
ALPHA = (2 * DEPTH) ** 0.25
LN_EPS = 1e-5

LANES = 128
SUBLANES = 8
V7X_VMEM_LIMIT_BYTES = 56 * 1024 * 1024

LOG2E = math.log2(math.e)
NEG_INF = float("-inf")
M_FLOOR = -3.0e38
INT_MIN = -(2 ** 31)

BQ = 256
CH = 128
TK = 1024
IDX_GROUP = 4
QK_ROWS = 128
PV_KEYS = 256
BAND = CH + BQ

S1_PAGES = 32
S3_PAGES = 8
SELECT_SEQS = 8


def _bucket_lower_bounds():
    max_exact = REL_BUCKETS // 2
    lows = [None] * REL_BUCKETS
    for d in range(0, REL_MAX_DIST + 1):
        if d < max_exact:
            b = d
        else:
            b = max_exact + int(math.log(d / max_exact) / math.log(REL_MAX_DIST / max_exact)
                                * (REL_BUCKETS - max_exact))
            b = min(b, REL_BUCKETS - 1)
        if lows[b] is None:
            lows[b] = d
    nxt = REL_MAX_DIST
    for b in range(REL_BUCKETS - 1, -1, -1):
        if lows[b] is None:
            lows[b] = nxt
        nxt = lows[b]
    return lows


BUCKET_LOW = _bucket_lower_bounds()


def _mm(a, b):
    return jnp.dot(a, b, preferred_element_type=F32)


def _mm_nt(a, b):
    return lax.dot_general(a, b, (((1,), (1,)), ((), ())), preferred_element_type=F32)


def _layer_norm(y, g, b):
    mu = jnp.mean(y, axis=-1, keepdims=True)
    d = y - mu
    var = jnp.mean(d * d, axis=-1, keepdims=True)
    return d * lax.rsqrt(var + LN_EPS) * g + b


def _params(n_axes):
    return pltpu.CompilerParams(dimension_semantics=("arbitrary",) * n_axes,
                                vmem_limit_bytes=V7X_VMEM_LIMIT_BYTES)


def _resident(shape):
    nd = len(shape)
    return pl.BlockSpec(shape, lambda *_: (0,) * nd, pipeline_mode=pl.Buffered(1))


def _row_tile(m):
    for t in (512, 256, 128, 64, 32, 16, 8):
        if m % t == 0:
            return t
    raise ValueError(f"row count {m} is not a multiple of 8")


def _bias_from_dist(dist, rb_ref, h):
    bias = jnp.full(dist.shape, rb_ref[0, h] * LOG2E, F32)
    for b in range(1, REL_BUCKETS):
        bias = jnp.where(dist >= BUCKET_LOW[b], rb_ref[b, h] * LOG2E, bias)
    return bias


def _key_to_f32(k):
    bits = jnp.where(k < 0, k ^ jnp.int32(0x7FFFFFFF), k)
    return lax.bitcast_convert_type(bits, F32)


def _key_hi16(x):
    bits = lax.bitcast_convert_type(x, I32)
    key = bits ^ ((bits >> 31) & jnp.int32(0x7FFFFFFF))
    return (key >> 16).astype(jnp.int16)


def _key_lo16(x):
    bits = lax.bitcast_convert_type(x, I32)
    key = bits ^ ((bits >> 31) & jnp.int32(0x7FFFFFFF))
    return ((key & jnp.int32(0xFFFF)) + I16_MIN).astype(jnp.int16)


I16_MIN = -(2 ** 15)
PACKED_ROWS = 2 * SUBLANES
KEY_NEG_INF = (0xFF800000 ^ 0x7FFFFFFF) - (1 << 32)
NEG_INF_HI16 = KEY_NEG_INF >> 16


def _glu_kernel(x_ref, w_ref, b_ref, u_ref):
    d = u_ref.shape[-1]
    xb = x_ref[...].astype(BF16)
    a = _mm(xb, w_ref[:, :d]) + b_ref[:, :d]
    g = _mm(xb, w_ref[:, d:]) + b_ref[:, d:]
    u_ref[...] = a * jax.nn.sigmoid(g)


def glu(x, w_bf, b):
    m, d = x.shape
    tm = _row_tile(m)
    return pl.pallas_call(
        _glu_kernel,
        grid=(m // tm,),
        in_specs=[pl.BlockSpec((tm, d), lambda i: (i, 0)), _resident((d, 2 * d)), _resident((1, 2 * d))],
        out_specs=pl.BlockSpec((tm, d), lambda i: (i, 0)),
        out_shape=jax.ShapeDtypeStruct((m, d), F32),
        compiler_params=_params(1),
        name="glu",
    )(x, w_bf, b.reshape(1, 2 * d))


CONV_HALO = 32
CONV_ROWS = 64
CONV_SEQS = 8


def _conv_prompt_kernel(u_ref, halo_ref, wdw_ref, bdw_ref, g_ref, b_ref, z_ref, win_ref, y_ref):
    i = pl.program_id(0)
    tm = u_ref.shape[0]
    ncb = win_ref.shape[0]
    halo = jnp.where(i > 0, halo_ref[...], 0.0)
    for cb in range(ncb):
        win_ref[cb, 0:CONV_HALO, :] = halo[:, cb * LANES:(cb + 1) * LANES]
        win_ref[cb, CONV_HALO:CONV_HALO + tm, :] = u_ref[:, cb * LANES:(cb + 1) * LANES]
    shift = CONV_HALO - CONV_STATE

    def cb_body(cb, carry):
        w = wdw_ref[cb]
        for r in range(tm // CONV_ROWS):
            acc = jnp.zeros((CONV_ROWS, LANES), F32)
            for j in range(CONV_WIDTH):
                acc = acc + win_ref[cb, pl.ds(r * CONV_ROWS + shift + j, CONV_ROWS), :] * w[j:j + 1, :]
            y_ref[cb, r * CONV_ROWS:(r + 1) * CONV_ROWS, :] = acc
        return carry

    lax.fori_loop(0, ncb, cb_body, 0)
    y = jnp.concatenate([y_ref[cb] for cb in range(ncb)], axis=1) + bdw_ref[...]
    z = _layer_norm(y, g_ref[...], b_ref[...])
    z_ref[...] = (z * jax.nn.sigmoid(z)).astype(BF16)


def conv_prompt(u, w_dw, b_dw, g_n, b_n):
    m, d = u.shape
    tm = _row_tile(m)
    ncb = d // LANES
    wdw = jnp.pad(w_dw, ((0, 1), (0, 0))).reshape(CONV_WIDTH + 1, ncb, LANES).transpose(1, 0, 2)
    hb = tm // CONV_HALO
    return pl.pallas_call(
        _conv_prompt_kernel,
        grid=(m // tm,),
        in_specs=[pl.BlockSpec((tm, d), lambda i: (i, 0)),
                  pl.BlockSpec((CONV_HALO, d), lambda i: (jnp.maximum(i * hb - 1, 0), 0)),
                  _resident((ncb, CONV_WIDTH + 1, LANES)),
                  _resident((1, d)), _resident((1, d)), _resident((1, d))],
        out_specs=pl.BlockSpec((tm, d), lambda i: (i, 0)),
        out_shape=jax.ShapeDtypeStruct((m, d), BF16),
        scratch_shapes=[pltpu.VMEM((ncb, tm + CONV_HALO, LANES), F32), pltpu.VMEM((ncb, tm, LANES), F32)],
        compiler_params=_params(1),
        name="conv_prompt",
    )(u, u, wdw, b_dw.reshape(1, d), g_n.reshape(1, d), b_n.reshape(1, d))


def _conv_sample_kernel(st_ref, u_ref, wdw_ref, bdw_ref, g_ref, b_ref, z_ref, ns_ref, win_ref):
    nb, t, _ = u_ref.shape
    for s in range(nb):
        win_ref[0:CONV_STATE, :] = st_ref[s]
        win_ref[CONV_STATE:CONV_STATE + t, :] = u_ref[s]
        acc = jnp.zeros(u_ref.shape[1:], F32)
        for j in range(CONV_WIDTH):
            acc = acc + win_ref[j:j + t, :] * wdw_ref[j:j + 1, :]
        z = _layer_norm(acc + bdw_ref[...], g_ref[...], b_ref[...])
        z_ref[s] = z * jax.nn.sigmoid(z)
        ns_ref[s] = win_ref[t:t + CONV_STATE, :]


def conv_sample(u, state, w_dw, b_dw, g_n, b_n):
    b, t, d = u.shape
    nb = math.gcd(b, CONV_SEQS)
    return pl.pallas_call(
        _conv_sample_kernel,
        grid=(b // nb,),
        in_specs=[pl.BlockSpec((nb, CONV_STATE, d), lambda i: (i, 0, 0)),
                  pl.BlockSpec((nb, t, d), lambda i: (i, 0, 0)),
                  _resident((CONV_WIDTH + 1, d)), _resident((1, d)), _resident((1, d)), _resident((1, d))],
        out_specs=[pl.BlockSpec((nb, t, d), lambda i: (i, 0, 0)),
                   pl.BlockSpec((nb, CONV_STATE, d), lambda i: (i, 0, 0))],
        out_shape=[jax.ShapeDtypeStruct((b, t, d), F32), jax.ShapeDtypeStruct((b, CONV_STATE, d), F32)],
        scratch_shapes=[pltpu.VMEM((CONV_STATE + t + 2, d), F32)],
        compiler_params=_params(1),
        name="conv_sample",
    )(state, u, jnp.pad(w_dw, ((0, 1), (0, 0))), b_dw.reshape(1, d), g_n.reshape(1, d), b_n.reshape(1, d))


def _proj_res_ln_kernel(z_ref, w_ref, bias_ref, x_ref, g_ref, b_ref, o_ref):
    m = _mm(z_ref[...].astype(BF16), w_ref[...]) + bias_ref[...]
    o_ref[...] = _layer_norm(ALPHA * x_ref[...] + m, g_ref[...], b_ref[...])


def proj_res_ln(z, w_bf, bias, x, g, b):
    m, d = x.shape
    k = z.shape[1]
    tm = _row_tile(m)
    return pl.pallas_call(
        _proj_res_ln_kernel,
        grid=(m // tm,),
        in_specs=[pl.BlockSpec((tm, k), lambda i: (i, 0)), _resident((k, d)), _resident((1, d)),
                  pl.BlockSpec((tm, d), lambda i: (i, 0)), _resident((1, d)), _resident((1, d))],
        out_specs=pl.BlockSpec((tm, d), lambda i: (i, 0)),
        out_shape=jax.ShapeDtypeStruct((m, d), F32),
        compiler_params=_params(1),
        name="proj_res_ln",
    )(z, w_bf, bias.reshape(1, d), x, g.reshape(1, d), b.reshape(1, d))


FFN_CHUNK = 256


def _ffn_kernel(x_ref, win_ref, wout_ref, g_ref, b_ref, o_ref):
    x = x_ref[...]
    xb = x.astype(BF16)
    dff = wout_ref.shape[0]
    acc = jnp.zeros(x.shape, F32)
    for c in range(dff // FFN_CHUNK):
        lo = c * FFN_CHUNK
        gate = _mm(xb, win_ref[:, lo:lo + FFN_CHUNK])
        up = _mm(xb, win_ref[:, dff + lo:dff + lo + FFN_CHUNK])
        act = (gate * jax.nn.sigmoid(gate) * up).astype(BF16)
        acc = acc + _mm(act, wout_ref[lo:lo + FFN_CHUNK, :])
    o_ref[...] = _layer_norm(ALPHA * x + acc, g_ref[...], b_ref[...])


def ffn(x, win_bf, wout_bf, g, b):
    m, d = x.shape
    dff = wout_bf.shape[0]
    tm = _row_tile(m)
    return pl.pallas_call(
        _ffn_kernel,
        grid=(m // tm,),
        in_specs=[pl.BlockSpec((tm, d), lambda i: (i, 0)), _resident((d, 2 * dff)), _resident((dff, d)),
                  _resident((1, d)), _resident((1, d))],
        out_specs=pl.BlockSpec((tm, d), lambda i: (i, 0)),
        out_shape=jax.ShapeDtypeStruct((m, d), F32),
        compiler_params=_params(1),
        name="ffn",
    )(x, win_bf, wout_bf, g.reshape(1, d), b.reshape(1, d))


Q_SCALE = HEAD_DIM ** -0.5 * LOG2E
W_SCALE = IDX_HEADS ** -0.5 * IDX_DIM ** -0.5
QI_PAD = IDX_HEADS * LANES
V_ROWS = HEAD_DIM + 16
VT_ROWS = N_HEADS * V_ROWS


def _split_attn_weights(w_in):
    d = D_MODEL
    o3 = 3 * d
    o4 = o3 + IDX_HEADS * IDX_DIM
    o5 = o4 + IDX_DIM
    wq, wk, wv = w_in[:, :d], w_in[:, d:2 * d], w_in[:, 2 * d:o3]
    wqi = w_in[:, o3:o4].reshape(d, IDX_HEADS, IDX_DIM)
    wqi = jnp.pad(wqi, ((0, 0), (0, 0), (0, LANES - IDX_DIM))).reshape(d, QI_PAD)
    wkw = jnp.pad(w_in[:, o4:], ((0, 0), (0, LANES - IDX_DIM - IDX_HEADS)))
    return tuple(a.astype(BF16) for a in (wq, wk, wv, wqi, wkw))


def _attn_proj_t_kernel(x_ref, wk_ref, wv_ref, wkw_ref, wqT_ref, wvT_ref, wqiT_ref, wwT_ref,
                        k32_ref, v32_ref, kiw_ref, qT_ref, kbf_ref, vT_ref, qiT_ref, kip_ref, wT_ref):
    xb = x_ref[...].astype(BF16)
    k = _mm(xb, wk_ref[...])
    k32_ref[...] = k
    kbf_ref[...] = k.astype(BF16)
    v32_ref[...] = _mm(xb, wv_ref[...])
    kiw = _mm(xb, wkw_ref[...])
    kiw_ref[...] = kiw
    lane = lax.broadcasted_iota(I32, kiw.shape, 1)
    kip_ref[...] = jnp.where(lane < IDX_DIM, kiw, 0.0).astype(BF16)
    qT_ref[...] = (_mm_nt(wqT_ref[...], xb) * Q_SCALE).astype(BF16)
    vt = _mm_nt(wvT_ref[...], xb).astype(BF16)
    ones = jnp.ones((V_ROWS - HEAD_DIM, vt.shape[1]), BF16)
    for h in range(N_HEADS):
        vT_ref[h * V_ROWS:h * V_ROWS + HEAD_DIM, :] = vt[h * HEAD_DIM:(h + 1) * HEAD_DIM]
        vT_ref[h * V_ROWS + HEAD_DIM:(h + 1) * V_ROWS, :] = ones
    qiT_ref[...] = _mm_nt(wqiT_ref[...], xb).astype(BF16)
    wT_ref[...] = _mm_nt(wwT_ref[...], xb) * W_SCALE


def attn_proj_t(x, w_in):
    m, d = x.shape
    tm = _row_tile(m)
    wq, wk, wv, wqi, wkw = _split_attn_weights(w_in)
    o5 = 3 * d + IDX_HEADS * IDX_DIM + IDX_DIM
    wwT = jnp.pad(w_in[:, o5:].T, ((0, 16 - IDX_HEADS), (0, 0))).astype(BF16)
    row = lambda n: pl.BlockSpec((tm, n), lambda i: (i, 0))
    col = lambda n: pl.BlockSpec((n, tm), lambda i: (0, i))
    return pl.pallas_call(
        _attn_proj_t_kernel,
        grid=(m // tm,),
        in_specs=[row(d), _resident((d, d)), _resident((d, d)), _resident((d, LANES)),
                  _resident((d, d)), _resident((d, d)), _resident((QI_PAD, d)), _resident((16, d))],
        out_specs=[row(d), row(d), row(LANES), col(d), row(d), col(VT_ROWS), col(QI_PAD), row(LANES), col(16)],
        out_shape=[jax.ShapeDtypeStruct((m, d), F32), jax.ShapeDtypeStruct((m, d), F32),
                   jax.ShapeDtypeStruct((m, LANES), F32), jax.ShapeDtypeStruct((d, m), BF16),
                   jax.ShapeDtypeStruct((m, d), BF16), jax.ShapeDtypeStruct((VT_ROWS, m), BF16),
                   jax.ShapeDtypeStruct((QI_PAD, m), BF16), jax.ShapeDtypeStruct((m, LANES), BF16),
                   jax.ShapeDtypeStruct((16, m), F32)],
        compiler_params=_params(1),
        name="attn_proj_t",
    )(x, wk, wv, wkw, wq.T, wv.T, wqi.T, wwT)


def _attn_proj_s_kernel(x_ref, wq_ref, wk_ref, wv_ref, wqi_ref, wkw_ref,
                        q_ref, k32_ref, v32_ref, qi_ref, kiw_ref):
    xb = x_ref[...].astype(BF16)
    q_ref[...] = (_mm(xb, wq_ref[...]) * Q_SCALE).astype(BF16)
    k32_ref[...] = _mm(xb, wk_ref[...])
    v32_ref[...] = _mm(xb, wv_ref[...])
    qi_ref[...] = _mm(xb, wqi_ref[...]).astype(BF16)
    kiw_ref[...] = _mm(xb, wkw_ref[...])


def attn_proj_s(x, w_in):
    m, d = x.shape
    tm = _row_tile(m)
    wq, wk, wv, wqi, wkw = _split_attn_weights(w_in)
    row = lambda n: pl.BlockSpec((tm, n), lambda i: (i, 0))
    return pl.pallas_call(
        _attn_proj_s_kernel,
        grid=(m // tm,),
        in_specs=[row(d), _resident((d, d)), _resident((d, d)), _resident((d, d)),
                  _resident((d, QI_PAD)), _resident((d, LANES))],
        out_specs=[row(d), row(d), row(d), row(QI_PAD), row(LANES)],
        out_shape=[jax.ShapeDtypeStruct((m, d), BF16), jax.ShapeDtypeStruct((m, d), F32),
                   jax.ShapeDtypeStruct((m, d), F32), jax.ShapeDtypeStruct((m, QI_PAD), BF16),
                   jax.ShapeDtypeStruct((m, LANES), F32)],
        compiler_params=_params(1),
        name="attn_proj_s",
    )(x, wq, wk, wv, wqi, wkw)


def _prompt_schedule(t):
    qb_of, kt_of, fl = [], [], []
    for qb in range(t // BQ):
        nb = -(-max(qb * BQ - CH, 0) // TK)
        qb_of.append(qb); kt_of.append(0); fl.append(1 | (2 if nb == 0 else 0))
        for kt in range(nb):
            qb_of.append(qb); kt_of.append(kt); fl.append(2 if kt == nb - 1 else 0)
    return (np.asarray(qb_of, np.int32), np.asarray(kt_of, np.int32), np.asarray(fl, np.int32))


def _attn_prompt_kernel(qb_ref, kt_ref, fl_ref,
                        qT_ref, qiT_ref, wT_ref, kip_ref, kb_ref, vTb_ref, kp_ref, kd_ref, vTp_ref, vTd_ref,
                        rb_ref, o_ref,
                        s_ref, h_ref, thr_ref, bb_ref, m_ref, l_ref, acc_ref, madd_ref, *, k_sel, idx_bits):
    step = pl.program_id(0)
    qb = qb_ref[step]
    kt = kt_ref[step]
    flags = fl_ref[step]
    is_band = (flags & 1) != 0
    is_last = (flags & 2) != 0
    q0 = qb * BQ
    nch = 2 * (qb + 1)
    cband = 2 * qb - 1
    kf = float(k_sel)

    @pl.when(step == 0)
    def _build_band_bias():
        r = lax.broadcasted_iota(I32, (BAND, BQ), 0)
        j = lax.broadcasted_iota(I32, (BAND, BQ), 1)
        dist = CH + j - r
        for h in range(N_HEADS):
            bb_ref[h] = _bias_from_dist(dist, rb_ref, h)

    def head_cols(h):
        return slice(h * HEAD_DIM, (h + 1) * HEAD_DIM)

    def head_vrows(h):
        return slice(h * V_ROWS, (h + 1) * V_ROWS)

    def qk(h, kh):
        n = kh.shape[0] // QK_ROWS
        return jnp.concatenate([_mm(kh[i * QK_ROWS:(i + 1) * QK_ROWS], qT_ref[head_cols(h), :])
                                for i in range(n)], axis=0)

    def softmax_pv(h, lt, vth, madd, bias, cbias):
        hs = head_cols(h)
        if bias is not None:
            lt = lt + bias
        lt = lt + madd
        mx = jnp.max(lt, axis=0, keepdims=True) + cbias
        m_old = m_ref[h:h + 1, :]
        m_new = jnp.maximum(m_old, mx)
        alpha = jnp.exp2(m_old - m_new)
        pb = jnp.exp2(lt - (m_new - cbias)).astype(BF16)
        r = pb.shape[0]
        if r % (2 * PV_KEYS) == 0:
            pv = _mm(vth[:, :r // 2], pb[:r // 2]) + _mm(vth[:, r // 2:], pb[r // 2:])
        else:
            pv = _mm(vth, pb)
        acc_ref[hs, :] = alpha * acc_ref[hs, :] + pv[0:HEAD_DIM]
        l_ref[h:h + 1, :] = alpha * l_ref[h:h + 1, :] + pv[HEAD_DIM:HEAD_DIM + 1]
        m_ref[h:h + 1, :] = m_new

    def attend_heads(keys_of, vals_of, madd, bias_of, cbias_of):
        lt = qk(0, keys_of(0))
        for h in range(N_HEADS):
            lt_next = qk(h + 1, keys_of(h + 1)) if h + 1 < N_HEADS else None
            softmax_pv(h, lt, vals_of(h), madd, bias_of(h), cbias_of(h))
            lt = lt_next

    @pl.when(is_band)
    def _band_step():
        def idx_body(gi, carry):
            rows = IDX_GROUP * CH
            kic = kip_ref[pl.ds(pl.multiple_of(gi * rows, rows), rows), :]
            acc = None
            for h in range(IDX_HEADS):
                sc = _mm(kic, qiT_ref[h * LANES:(h + 1) * LANES, :])
                val = jnp.maximum(sc, 0.0) * wT_ref[h:h + 1, :]
                acc = val if acc is None else acc + val
            acc = jnp.where(acc == 0.0, 0.0, acc)
            hi = _key_hi16(acc)
            for i in range(IDX_GROUP):
                s_ref[gi * IDX_GROUP + i] = acc[i * CH:(i + 1) * CH]
                h_ref[gi * IDX_GROUP + i] = hi[i * CH:(i + 1) * CH]
            return carry

        ngrp = (nch + IDX_GROUP - 1) // IDX_GROUP
        lax.fori_loop(0, ngrp, idx_body, 0)
        rr = lax.broadcasted_iota(I32, (CH, BQ), 0)
        jj = lax.broadcasted_iota(I32, (CH, BQ), 1)
        for i in range(BQ // CH):
            c = 2 * qb + i
            masked = jnp.where(i * CH + rr <= jj, s_ref[c], NEG_INF)
            s_ref[c] = masked
            h_ref[c] = _key_hi16(masked)

        @pl.when(ngrp * IDX_GROUP > nch)
        def _mask_group_tail():
            for c in range(IDX_GROUP - BQ // CH):
                s_ref[nch + c] = jnp.full((CH, BQ), NEG_INF, F32)
                h_ref[nch + c] = jnp.full((CH, BQ), NEG_INF_HI16, jnp.int16)

        def count(pred):
            def body(g, acc):
                for c in range(IDX_GROUP):
                    ci = g * IDX_GROUP + c
                    ind = jnp.where(pred(ci, s_ref[ci]), 1.0, 0.0)
                    for r in range(CH // SUBLANES):
                        acc = acc + ind[r * SUBLANES:(r + 1) * SUBLANES]
                return acc
            acc = lax.fori_loop(0, ngrp, body, jnp.zeros((SUBLANES, BQ), F32))
            return jnp.broadcast_to(jnp.sum(acc, axis=0, keepdims=True), (SUBLANES, BQ))

        def count_hi(pred):
            one, zero = jnp.ones((), jnp.int16), jnp.zeros((), jnp.int16)

            def body(g, acc):
                for c in range(IDX_GROUP):
                    ind = jnp.where(pred(h_ref[g * IDX_GROUP + c]), one, zero)
                    for r in range(CH // PACKED_ROWS):
                        acc = acc + ind[r * PACKED_ROWS:(r + 1) * PACKED_ROWS]
                return acc
            acc = lax.fori_loop(0, ngrp, body, jnp.zeros((PACKED_ROWS, BQ), jnp.int16))
            tot = jnp.sum(acc.astype(I32).astype(F32), axis=0, keepdims=True)
            return jnp.broadcast_to(tot, (SUBLANES, BQ))

        def row16(v):
            return jnp.broadcast_to(v[0:1, :], (PACKED_ROWS, BQ)).astype(jnp.int16)[0:1, :]

        def search16(cnt_min, target, early_exit):
            c0 = count_hi(lambda x: x >= jnp.zeros((), jnp.int16))
            ok0 = jnp.logical_and(c0 >= target, cnt_min != target)
            ans = jnp.where(ok0, 0, I16_MIN).astype(I32)
            cnt = jnp.where(ok0, c0, cnt_min)

            def bit_step(i, ans, cnt):
                cand = ans + jnp.left_shift(jnp.int32(1), 14 - i)
                c16 = row16(cand)
                c = count_hi(lambda x: x >= c16)
                ok = jnp.logical_and(c >= target, cnt != target)
                return jnp.where(ok, cand, ans), jnp.where(ok, c, cnt)

            if not early_exit:
                return lax.fori_loop(0, 15, lambda i, c: bit_step(i, *c), (ans, cnt))

            ans, cnt = bit_step(jnp.int32(0), ans, cnt)

            def cond(carry):
                j, _, cnt = carry
                return jnp.logical_and(j < 7, jnp.max(jnp.abs(cnt - target)) > 0.5)

            def body(carry):
                j, ans, cnt = carry
                ans, cnt = bit_step(2 * j + 1, ans, cnt)
                ans, cnt = bit_step(2 * j + 2, ans, cnt)
                return j + 1, ans, cnt

            _, ans, cnt = lax.while_loop(cond, body, (jnp.int32(0), ans, cnt))
            return ans, cnt

        total = (ngrp * (IDX_GROUP * CH)).astype(F32)
        k_vec = jnp.full((SUBLANES, BQ), kf, F32)
        ans_hi, cnt_hi = search16(jnp.broadcast_to(total, (SUBLANES, BQ)), k_vec, early_exit=False)
        p16 = row16(ans_hi)
        n_above = count_hi(lambda x: x > p16)
        def lo_body(g, carry):
            for c in range(IDX_GROUP):
                ci = g * IDX_GROUP + c
                h_ref[ci] = jnp.where(h_ref[ci] == p16, _key_lo16(s_ref[ci]), jnp.int16(I16_MIN))
            return carry

        lax.fori_loop(0, ngrp, lo_body, 0)
        ans_lo, cnt_lo = search16(cnt_hi - n_above, k_vec - n_above, early_exit=True)
        ans = ans_hi * (2 ** 16) + (ans_lo - I16_MIN)
        cnt = n_above + cnt_lo
        none = ans <= KEY_NEG_INF
        thr = jnp.where(none, NEG_INF, _key_to_f32(ans))
        thr_ref[...] = thr

        has = jnp.logical_and(cnt > kf, jnp.logical_not(none))
        any_ties = jnp.max(jnp.where(has, 1.0, 0.0)) > 0.5

        @pl.when(any_ties)
        def _resolve_ties():
            t1 = thr[0:1, :]
            n_gt = count(lambda c, x: x > t1)
            need = jnp.where(has, kf - n_gt, 3.0e38)
            rows = lax.broadcasted_iota(I32, (CH, BQ), 0)

            def j_body(i, j):
                cj = j + jnp.left_shift(jnp.int32(1), idx_bits - 1 - i)
                c1 = cj[0:1, :]
                f = count(lambda c, x: jnp.logical_and(x == t1, c * CH + rows < c1))
                return jnp.where(f < need, cj, j)

            jstar = lax.fori_loop(0, idx_bits, j_body, jnp.zeros((SUBLANES, BQ), I32))
            j1 = jstar[0:1, :]
            h1 = has[0:1, :]

            def demote(c, carry):
                x = s_ref[c]
                drop = jnp.logical_and(jnp.logical_and(x == t1, c * CH + rows > j1), h1)
                s_ref[c] = jnp.where(drop, NEG_INF, x)
                return carry

            lax.fori_loop(0, nch, demote, 0)

        m_ref[...] = jnp.full(m_ref.shape, M_FLOOR, F32)
        l_ref[...] = jnp.zeros(l_ref.shape, F32)
        acc_ref[...] = jnp.zeros(acc_ref.shape, F32)
        thr1 = thr_ref[0:1, :]
        cprev = jnp.maximum(cband, 0)
        sc = jnp.concatenate([s_ref[cprev], s_ref[2 * qb], s_ref[2 * qb + 1]], axis=0)
        r = lax.broadcasted_iota(I32, (BAND, BQ), 0)
        j = lax.broadcasted_iota(I32, (BAND, BQ), 1)
        ok = jnp.logical_and(sc >= thr1, r - CH <= j)
        ok = jnp.logical_and(ok, jnp.logical_or(r >= CH, qb > 0))
        madd = jnp.where(ok, 0.0, NEG_INF)
        attend_heads(lambda h: jnp.concatenate([kp_ref[:, head_cols(h)], kd_ref[:, head_cols(h)]], axis=0),
                     lambda h: jnp.concatenate([vTp_ref[head_vrows(h), :], vTd_ref[head_vrows(h), :]], axis=1),
                     madd, lambda h: bb_ref[h], lambda h: 0.0)

    @pl.when(jnp.logical_not(is_band))
    def _bulk_step():
        thr1 = thr_ref[0:1, :]
        for i in range(TK // CH):
            c = kt * (TK // CH) + i
            x = s_ref[jnp.minimum(c, cband - 1)]
            sel = jnp.where(x >= thr1, 0.0, NEG_INF)
            madd_ref[i * CH:(i + 1) * CH, :] = jnp.where(c < cband, sel, NEG_INF)
        attend_heads(lambda h: kb_ref[:, head_cols(h)], lambda h: vTb_ref[head_vrows(h), :], madd_ref[...],
                     lambda h: None, lambda h: rb_ref[REL_BUCKETS - 1, h] * LOG2E)

    @pl.when(is_last)
    def _finalize():
        for h in range(N_HEADS):
            hs = slice(h * HEAD_DIM, (h + 1) * HEAD_DIM)
            lsum = l_ref[h:h + 1, :]
            o_ref[:, hs] = (acc_ref[hs, :] / lsum).T.astype(BF16)


def attn_prompt(qT, qiT, wT, kip, kbf, vT, rel_bias):
    d, t = qT.shape
    k_sel = min(TOPK_MAX, t // 4)
    qb_of, kt_of, fl = _prompt_schedule(t)
    n_steps = len(qb_of)
    tk = min(TK, t)
    qcol = lambda n: pl.BlockSpec((n, BQ), lambda s, qb, kt, fl: (0, qb[s]))
    in_specs = [
        qcol(d), qcol(QI_PAD), qcol(16),
        pl.BlockSpec((t, LANES), lambda s, qb, kt, fl: (0, 0), pipeline_mode=pl.Buffered(1)),
        pl.BlockSpec((tk, d), lambda s, qb, kt, fl: (kt[s], 0)),
        pl.BlockSpec((VT_ROWS, tk), lambda s, qb, kt, fl: (0, kt[s])),
        pl.BlockSpec((CH, d), lambda s, qb, kt, fl: (jnp.maximum(2 * qb[s] - 1, 0), 0)),
        pl.BlockSpec((BQ, d), lambda s, qb, kt, fl: (qb[s], 0)),
        pl.BlockSpec((VT_ROWS, CH), lambda s, qb, kt, fl: (0, jnp.maximum(2 * qb[s] - 1, 0))),
        pl.BlockSpec((VT_ROWS, BQ), lambda s, qb, kt, fl: (0, qb[s])),
        pl.BlockSpec(memory_space=pltpu.SMEM),
    ]
    grid_spec = pltpu.PrefetchScalarGridSpec(
        num_scalar_prefetch=3,
        grid=(n_steps,),
        in_specs=in_specs,
        out_specs=pl.BlockSpec((BQ, d), lambda s, qb, kt, fl: (qb[s], 0)),
        scratch_shapes=[
            pltpu.VMEM((t // CH, CH, BQ), F32),
            pltpu.VMEM((t // CH, CH, BQ), jnp.int16),
            pltpu.VMEM((SUBLANES, BQ), F32),
            pltpu.VMEM((N_HEADS, BAND, BQ), F32),
            pltpu.VMEM((N_HEADS, BQ), F32),
            pltpu.VMEM((N_HEADS, BQ), F32),
            pltpu.VMEM((d, BQ), F32),
            pltpu.VMEM((tk, BQ), F32),
        ],
    )
    kern = functools.partial(_attn_prompt_kernel, k_sel=k_sel, idx_bits=max(t.bit_length() - 1, 1))
    return pl.pallas_call(
        kern,
        grid_spec=grid_spec,
        out_shape=jax.ShapeDtypeStruct((t, d), BF16),
        compiler_params=_params(1),
        name="attn_prompt",
    )(jnp.asarray(qb_of), jnp.asarray(kt_of), jnp.asarray(fl),
      qT, qiT, wT, kip, kbf, vT, kbf, kbf, vT, vT, rel_bias)


def _sample_scores_kernel(pt_ref, qi_ref, w_ref, *rest, n_groups):
    pages = rest[:S1_PAGES]
    kin_ref, o_ref, kbuf_ref = rest[S1_PAGES:]
    b = pl.program_id(0)
    g = pl.program_id(1)
    t = o_ref.shape[1]

    @pl.when(jnp.logical_and(b == 0, g == 0))
    def _zero_pad_rows():
        kbuf_ref[...] = jnp.zeros(kbuf_ref.shape, BF16)

    def head_sum(val):
        s = val[0:t]
        for h in range(1, IDX_HEADS):
            s = s + val[h * t:(h + 1) * t]
        return s

    wcol = w_ref[:, 0:1]

    @pl.when(g < n_groups)
    def _past_pages():
        for i in range(S1_PAGES):
            kbuf_ref[0:IDX_DIM, i * PAGE_SIZE:(i + 1) * PAGE_SIZE] = pages[i][...].astype(BF16)
        s = head_sum(jnp.maximum(_mm(qi_ref[...], kbuf_ref[...]), 0.0) * wcol)
        for i in range(S1_PAGES):
            o_ref[i] = s[:, i * PAGE_SIZE:(i + 1) * PAGE_SIZE]

    @pl.when(g == n_groups)
    def _new_keys():
        s = head_sum(jnp.maximum(_mm_nt(qi_ref[...], kin_ref[...]), 0.0) * wcol)
        row = lax.broadcasted_iota(I32, s.shape, 0)
        lane = lax.broadcasted_iota(I32, s.shape, 1)
        o_ref[0] = jnp.where(lane <= row, s, NEG_INF)
        for i in range(1, S1_PAGES):
            o_ref[i] = jnp.full(s.shape, NEG_INF, F32)


def sample_scores(page_table, qi_ht, w_ht, cache_kidx, layer, ki_new):
    b, n_pages = page_table.shape
    t = qi_ht.shape[1] // IDX_HEADS
    n_groups = n_pages // S1_PAGES
    page_spec = lambda i: pl.BlockSpec(
        (None, None, IDX_DIM, PAGE_SIZE),
        lambda bi, g, pt: (layer, pt[bi, jnp.minimum(g, n_groups - 1) * S1_PAGES + i], 0, 0))
    grid_spec = pltpu.PrefetchScalarGridSpec(
        num_scalar_prefetch=1,
        grid=(b, n_groups + 1),
        in_specs=[pl.BlockSpec((None, IDX_HEADS * t, LANES), lambda bi, g, pt: (bi, 0, 0)),
                  pl.BlockSpec((None, IDX_HEADS * t, LANES), lambda bi, g, pt: (bi, 0, 0))]
                 + [page_spec(i) for i in range(S1_PAGES)]
                 + [pl.BlockSpec((None, PAGE_SIZE, LANES), lambda bi, g, pt: (bi, 0, 0))],
        out_specs=pl.BlockSpec((None, S1_PAGES, t, LANES), lambda bi, g, pt: (bi, g, 0, 0)),
        scratch_shapes=[pltpu.VMEM((LANES, S1_PAGES * PAGE_SIZE), BF16)],
    )
    kidx_t = jnp.swapaxes(cache_kidx, 2, 3)
    return pl.pallas_call(
        functools.partial(_sample_scores_kernel, n_groups=n_groups),
        grid_spec=grid_spec,
        out_shape=jax.ShapeDtypeStruct((b, (n_groups + 1) * S1_PAGES, t, LANES), F32),
        compiler_params=_params(2),
        name="sample_scores",
    )(page_table, qi_ht, w_ht, *([kidx_t] * S1_PAGES), ki_new)


def _sample_select_kernel(s_ref, o_ref, *, k_sel, past, idx_bits):
    x = s_ref[...]
    nb, nck, t, _ = x.shape
    kf = float(k_sel)
    cidx = lax.broadcasted_iota(I32, x.shape, 1)
    row = lax.broadcasted_iota(I32, x.shape, 2)
    lane = lax.broadcasted_iota(I32, x.shape, 3)
    idx = cidx * LANES + lane

    def count(pred):
        per_lane = jnp.sum(jnp.where(pred, 1.0, 0.0), axis=1)
        return jnp.broadcast_to(jnp.sum(per_lane, axis=2, keepdims=True), (nb, t, LANES))

    def count_ge(cand_key):
        return count(x >= _key_to_f32(cand_key)[:, None])

    total = float(nck * LANES)
    c0 = count(x >= 0.0)
    ans = jnp.where(c0 >= kf, 0, INT_MIN).astype(I32)
    cnt = jnp.where(c0 >= kf, c0, total)

    def bit_cond(carry):
        i, _, cnt = carry
        return jnp.logical_and(i < 31, jnp.max(jnp.abs(cnt - kf)) > 0.5)

    def bit_body(carry):
        i, ans, cnt = carry
        cand = ans + jnp.left_shift(jnp.int32(1), 30 - i)
        c = count_ge(cand)
        ok = jnp.logical_and(c >= kf, cnt != kf)
        return i + 1, jnp.where(ok, cand, ans), jnp.where(ok, c, cnt)

    _, ans, cnt = lax.while_loop(bit_cond, bit_body, (jnp.int32(0), ans, cnt))
    none = ans == INT_MIN
    thr = jnp.where(none, NEG_INF, _key_to_f32(ans))
    has = jnp.logical_and(cnt > kf, jnp.logical_not(none))
    any_ties = jnp.max(jnp.where(has, 1.0, 0.0)) > 0.5
    new = idx - past
    valid = jnp.logical_or(idx < past, jnp.logical_and(new < t, new <= row))
    keep = jnp.logical_and(x >= thr[:, None], valid)

    @pl.when(jnp.logical_not(any_ties))
    def _no_ties():
        o_ref[...] = jnp.where(keep, 0.0, NEG_INF)

    @pl.when(any_ties)
    def _resolve_ties():
        n_gt = count(x > thr[:, None])
        need = jnp.where(has, kf - n_gt, 3.0e38)
        eq = x == thr[:, None]

        def j_body(i, j):
            cj = j + jnp.left_shift(jnp.int32(1), idx_bits - 1 - i)
            f = count(jnp.logical_and(eq, idx < cj[:, None]))
            return jnp.where(f < need, cj, j)

        jstar = lax.fori_loop(0, idx_bits, j_body, jnp.zeros((nb, t, LANES), I32))
        drop = jnp.logical_and(jnp.logical_and(eq, idx > jstar[:, None]), has[:, None])
        o_ref[...] = jnp.where(jnp.logical_and(keep, jnp.logical_not(drop)), 0.0, NEG_INF)


def sample_select(scores, past, k_sel):
    b, nck, t, _ = scores.shape
    nb = math.gcd(b, SELECT_SEQS)
    spec = pl.BlockSpec((nb, nck, t, LANES), lambda i: (i, 0, 0, 0))
    kern = functools.partial(_sample_select_kernel, k_sel=k_sel, past=past,
                             idx_bits=(nck * LANES - 1).bit_length())
    return pl.pallas_call(
        kern, grid=(b // nb,), in_specs=[spec], out_specs=spec,
        out_shape=jax.ShapeDtypeStruct(scores.shape, F32),
        compiler_params=_params(1),
        name="sample_select",
    )(scores)


def _sample_attn_kernel(pt_ref, q_ref, madd_ref, *rest, n_groups, t):
    kpages = rest[:S3_PAGES]
    vpages = rest[S3_PAGES:2 * S3_PAGES]
    (kn_ref, vn_ref, rb_ref, o_ref,
     kcat_ref, vcat_ref, m_ref, l_ref, acc_ref, bconst_ref, blast_ref, bnew_ref) = rest[2 * S3_PAGES:]
    b = pl.program_id(0)
    g = pl.program_id(1)
    rows = N_HEADS * t

    @pl.when(jnp.logical_and(b == 0, g == 0))
    def _build_bias_tables():
        tq = lax.broadcasted_iota(I32, (t, LANES), 0)
        ln = lax.broadcasted_iota(I32, (t, LANES), 1)
        for h in range(N_HEADS):
            hs = slice(h * t, (h + 1) * t)
            bconst_ref[hs, :] = jnp.full((t, LANES), rb_ref[REL_BUCKETS - 1, h] * LOG2E, F32)
            blast_ref[hs, :] = _bias_from_dist(PAGE_SIZE + tq - ln, rb_ref, h)
            bnew_ref[hs, :] = _bias_from_dist(tq - ln, rb_ref, h)

    @pl.when(g == 0)
    def _init():
        m_ref[...] = jnp.full(m_ref.shape, M_FLOOR, F32)
        l_ref[...] = jnp.zeros(l_ref.shape, F32)
        acc_ref[...] = jnp.zeros(acc_ref.shape, F32)

    def tile_heads(m8):
        return jnp.concatenate([m8] * N_HEADS, axis=0)

    def attend(kk, vv, bias, madd):
        lg = _mm_nt(q_ref[...], kk) + bias + madd
        mx = jnp.max(lg, axis=1, keepdims=True)
        m_old = m_ref[...]
        m_new = jnp.maximum(m_old, mx)
        alpha = jnp.exp2(m_old - m_new)
        p = jnp.exp2(lg - m_new[:, 0:1])
        l_ref[...] = alpha * l_ref[...] + jnp.sum(p, axis=1, keepdims=True)
        acc_ref[...] = alpha[:, 0:1] * acc_ref[...] + _mm(p.astype(BF16), vv)
        m_ref[...] = m_new

    @pl.when(g < n_groups)
    def _cached_pages():
        for i in range(S3_PAGES):
            rs = slice(i * PAGE_SIZE, (i + 1) * PAGE_SIZE)
            for h in range(N_HEADS):
                hs = slice(h * HEAD_DIM, (h + 1) * HEAD_DIM)
                head_rows = pl.ds(h, PAGE_SIZE, stride=N_HEADS)
                kcat_ref[rs, hs] = kpages[i][head_rows, :].astype(BF16)
                vcat_ref[rs, hs] = vpages[i][head_rows, :].astype(BF16)
        last = jnp.where(g == n_groups - 1, blast_ref[...], bconst_ref[...])
        bias = jnp.concatenate([bconst_ref[...]] * (S3_PAGES - 1) + [last], axis=1)
        madd = jnp.concatenate([tile_heads(madd_ref[i]) for i in range(S3_PAGES)], axis=1)
        attend(kcat_ref[...], vcat_ref[...], bias, madd)

    @pl.when(g == n_groups)
    def _new_keys_and_finish():
        attend(kn_ref[...], vn_ref[...], bnew_ref[...], tile_heads(madd_ref[0]))
        for h in range(N_HEADS):
            hs = slice(h * HEAD_DIM, (h + 1) * HEAD_DIM)
            rs = slice(h * t, (h + 1) * t)
            o_ref[:, hs] = acc_ref[rs, hs] / l_ref[rs, 0:1]


def sample_attn(page_table, q_bd, madd, cache_k, cache_v, layer, k_new, v_new, rel_bias):
    b, n_pages = page_table.shape
    rows, d = q_bd.shape[1:]
    t = rows // N_HEADS
    n_groups = n_pages // S3_PAGES
    page_spec = lambda i: pl.BlockSpec(
        (None, None, PAGE_SIZE * N_HEADS, HEAD_DIM),
        lambda bi, g, pt: (layer, pt[bi, jnp.minimum(g, n_groups - 1) * S3_PAGES + i], 0, 0))
    per_b = lambda shape: pl.BlockSpec((None,) + shape, lambda bi, g, pt: (bi,) + (0,) * len(shape))
    grid_spec = pltpu.PrefetchScalarGridSpec(
        num_scalar_prefetch=1,
        grid=(b, n_groups + 1),
        in_specs=[per_b((rows, d)),
                  pl.BlockSpec((None, S3_PAGES, t, LANES), lambda bi, g, pt: (bi, g, 0, 0))]
                 + [page_spec(i) for i in range(S3_PAGES)] * 2
                 + [per_b((PAGE_SIZE, d)), per_b((PAGE_SIZE, d)), pl.BlockSpec(memory_space=pltpu.SMEM)],
        out_specs=per_b((t, d)),
        scratch_shapes=[pltpu.VMEM((S3_PAGES * PAGE_SIZE, d), BF16), pltpu.VMEM((S3_PAGES * PAGE_SIZE, d), BF16),
                        pltpu.VMEM((rows, LANES), F32), pltpu.VMEM((rows, LANES), F32),
                        pltpu.VMEM((rows, d), F32),
                        pltpu.VMEM((rows, LANES), F32), pltpu.VMEM((rows, LANES), F32),
                        pltpu.VMEM((rows, LANES), F32)],
    )
    return pl.pallas_call(
        functools.partial(_sample_attn_kernel, n_groups=n_groups, t=t),
        grid_spec=grid_spec,
        out_shape=jax.ShapeDtypeStruct((b, t, d), F32),
        compiler_params=_params(2),
        name="sample_attn",
    )(page_table, q_bd, madd, *([cache_k] * S3_PAGES), *([cache_v] * S3_PAGES), k_new, v_new, rel_bias)


def attn_sample(x, cache_k, cache_v, cache_kidx, page_table, layer, w_in, rel_bias):
    b, t, d = x.shape
    past = page_table.shape[1] * PAGE_SIZE
    k_sel = min(TOPK_MAX, (past + t) // 4)
    q, k32, v32, qi, kiw = attn_proj_s(x.reshape(b * t, d), w_in)
    qi_ht = qi.reshape(b, t, IDX_HEADS, LANES).transpose(0, 2, 1, 3).reshape(b, IDX_HEADS * t, LANES)
    wi = kiw[:, IDX_DIM:IDX_DIM + IDX_HEADS] * W_SCALE
    w_ht = jnp.broadcast_to(wi.reshape(b, t, IDX_HEADS).transpose(0, 2, 1).reshape(b, IDX_HEADS * t, 1),
                            (b, IDX_HEADS * t, LANES))
    head_of_col = jnp.arange(d, dtype=I32) // HEAD_DIM
    q_bd = jnp.where(head_of_col[None, None, None, :] == jnp.arange(N_HEADS, dtype=I32)[None, :, None, None],
                     q.reshape(b, 1, t, d), jnp.zeros((), BF16)).reshape(b, N_HEADS * t, d)
    pad_rows = lambda a: jnp.pad(a.reshape(b, t, -1), ((0, 0), (0, PAGE_SIZE - t), (0, 0))).astype(BF16)
    ki_new = pad_rows(jnp.pad(kiw[:, :IDX_DIM], ((0, 0), (0, LANES - IDX_DIM))))
    scores = sample_scores(page_table, qi_ht, w_ht, cache_kidx, layer, ki_new)
    madd = sample_select(scores, past, k_sel)
    page_rows = lambda c: c.reshape(c.shape[0], c.shape[1], PAGE_SIZE * N_HEADS, HEAD_DIM)
    out = sample_attn(page_table, q_bd, madd, page_rows(cache_k), page_rows(cache_v), layer,
                      pad_rows(k32), pad_rows(v32), rel_bias)
    return out.reshape(b * t, d), k32, v32, kiw[:, :IDX_DIM]


def kernel(x_prompt, x_sample, state_conv, cache_k, cache_v, cache_kidx, page_table, rel_bias,
           w_pw1, b_pw1, w_dw, b_dw, conv_norm_g, conv_norm_b, w_pw2, b_pw2,
           w_attn_in, w_attn_out, w_ffn_in, w_ffn_out, ln_mix_g, ln_mix_b, ln_ffn_g, ln_ffn_b):
    bp, tp, d = x_prompt.shape
    bs, ts, _ = x_sample.shape
    assert bp == 1 and d == D_MODEL and tp % TK == 0
    xp = x_prompt.reshape(tp, d)
    xs = x_sample.reshape(bs * ts, d)
    bf = lambda a: a.astype(BF16)

    w1, w2 = bf(w_pw1[0]), bf(w_pw2[0])
    up = glu(xp, w1, b_pw1[0])
    us = glu(xs, w1, b_pw1[0])
    zp = conv_prompt(up, w_dw[0], b_dw[0], conv_norm_g[0], conv_norm_b[0])
    zs, conv_s = conv_sample(us.reshape(bs, ts, d), state_conv[0], w_dw[0], b_dw[0],
                             conv_norm_g[0], conv_norm_b[0])
    conv_p = up[tp - CONV_STATE:]
    xp = proj_res_ln(zp, w2, b_pw2[0], xp, ln_mix_g[0], ln_mix_b[0])
    xs = proj_res_ln(zs.reshape(bs * ts, d), w2, b_pw2[0], xs, ln_mix_g[0], ln_mix_b[0])
    wf_in, wf_out = bf(w_ffn_in[0]), bf(w_ffn_out[0])
    xp = ffn(xp, wf_in, wf_out, ln_ffn_g[0], ln_ffn_b[0])
    xs = ffn(xs, wf_in, wf_out, ln_ffn_g[0], ln_ffn_b[0])

    w_o = bf(w_attn_out[0])
    no_bias = jnp.zeros((d,), F32)
    kp32, vp32, kiwp, qT, kbf, vT, qiT, kip, wT = attn_proj_t(xp, w_attn_in[0])
    ap = attn_prompt(qT, qiT, wT, kip, kbf, vT, rel_bias)
    xp = proj_res_ln(ap, w_o, no_bias, xp, ln_mix_g[1], ln_mix_b[1])
    a_s, ks32, vs32, kis = attn_sample(xs.reshape(bs, ts, d), cache_k, cache_v, cache_kidx, page_table, 0,
                                       w_attn_in[0], rel_bias)
    xs = proj_res_ln(a_s, w_o, no_bias, xs, ln_mix_g[1], ln_mix_b[1])
    wf_in, wf_out = bf(w_ffn_in[1]), bf(w_ffn_out[1])
    xp = ffn(xp, wf_in, wf_out, ln_ffn_g[1], ln_ffn_b[1])
    xs = ffn(xs, wf_in, wf_out, ln_ffn_g[1], ln_ffn_b[1])

    return (xp.reshape(1, tp, d), xs.reshape(bs, ts, d),
            conv_p.reshape(1, 1, CONV_STATE, d), conv_s.reshape(1, bs, CONV_STATE, d),
            kp32.reshape(1, 1, tp, N_HEADS, HEAD_DIM), vp32.reshape(1, 1, tp, N_HEADS, HEAD_DIM),
            kiwp[:, :IDX_DIM].reshape(1, 1, tp, IDX_DIM),
            ks32.reshape(1, bs, ts, N_HEADS, HEAD_DIM), vs32.reshape(1, bs, ts, N_HEADS, HEAD_DIM),
            kis.reshape(1, bs, ts, IDX_DIM))
```

```python
import functools
import math

import numpy as np
import jax
import jax.numpy as jnp
from jax import lax
from jax.experimental import pallas as pl
from jax.experimental.pallas import tpu as pltpu

F32, BF16, I32 = jnp.float32, jnp.bfloat16, jnp.int32

D_MODEL = 1024
N_HEADS = 8
HEAD_DIM = D_MODEL // N_HEADS
IDX_HEADS = 8
IDX_DIM = 64
TOPK_MAX = 256
CONV_WIDTH = 31
CONV_STATE = CONV_WIDTH - 1
D_FF = 2816
REL_BUCKETS = 32
REL_MAX_DIST = 128
PAGE_SIZE = 128
DEPTH = 2
ALPHA = (2 * DEPTH) ** 0.25
LN_EPS = 1e-5

LANES = 128
SUBLANES = 8
V7X_VMEM_LIMIT_BYTES = 56 * 1024 * 1024

LOG2E = math.log2(math.e)
NEG_INF = float("-inf")
M_FLOOR = -3.0e38
INT_MIN = -(2 ** 31)

BQ = 256
CH = 128
TK = 1024
IDX_GROUP = 4
QK_ROWS = 128
PV_KEYS = 256
BAND = CH + BQ

S1_PAGES = 32
S3_PAGES = 16
SELECT_SEQS = 8


def _bucket_lower_bounds():
    max_exact = REL_BUCKETS // 2
    lows = [None] * REL_BUCKETS
    for d in range(0, REL_MAX_DIST + 1):
        if d < max_exact:
            b = d
        else:
            b = max_exact + int(math.log(d / max_exact) / math.log(REL_MAX_DIST / max_exact)
                                * (REL_BUCKETS - max_exact))
            b = min(b, REL_BUCKETS - 1)
        if lows[b] is None:
            lows[b] = d
    nxt = REL_MAX_DIST
    for b in range(REL_BUCKETS - 1, -1, -1):
        if lows[b] is None:
            lows[b] = nxt
        nxt = lows[b]
    return lows


BUCKET_LOW = _bucket_lower_bounds()


def _mm(a, b):
    return jnp.dot(a, b, preferred_element_type=F32)


def _mm_nt(a, b):
    return lax.dot_general(a, b, (((1,), (1,)), ((), ())), preferred_element_type=F32)


def _layer_norm(y, g, b):
    mu = jnp.mean(y, axis=-1, keepdims=True)
    d = y - mu
    var = jnp.mean(d * d, axis=-1, keepdims=True)
    return d * lax.rsqrt(var + LN_EPS) * g + b


def _params(n_axes):
    return pltpu.CompilerParams(dimension_semantics=("arbitrary",) * n_axes,
                                vmem_limit_bytes=V7X_VMEM_LIMIT_BYTES)


def _resident(shape):
    nd = len(shape)
    return pl.BlockSpec(shape, lambda *_: (0,) * nd, pipeline_mode=pl.Buffered(1))


def _row_tile(m):
    for t in (512, 256, 128, 64, 32, 16, 8):
        if m % t == 0:
            return t
    raise ValueError(f"row count {m} is not a multiple of 8")


def _bias_from_dist(dist, rb_ref, h):
    bias = jnp.full(dist.shape, rb_ref[0, h] * LOG2E, F32)
    for b in range(1, REL_BUCKETS):
        bias = jnp.where(dist >= BUCKET_LOW[b], rb_ref[b, h] * LOG2E, bias)
    return bias


def _key_to_f32(k):
    bits = jnp.where(k < 0, k ^ jnp.int32(0x7FFFFFFF), k)
    return lax.bitcast_convert_type(bits, F32)


def _key_hi16(x):
    bits = lax.bitcast_convert_type(x, I32)
    key = bits ^ ((bits >> 31) & jnp.int32(0x7FFFFFFF))
    return (key >> 16).astype(jnp.int16)


def _key_lo16(x):
    bits = lax.bitcast_convert_type(x, I32)
    key = bits ^ ((bits >> 31) & jnp.int32(0x7FFFFFFF))
    return ((key & jnp.int32(0xFFFF)) + I16_MIN).astype(jnp.int16)


I16_MIN = -(2 ** 15)
PACKED_ROWS = 2 * SUBLANES
KEY_NEG_INF = (0xFF800000 ^ 0x7FFFFFFF) - (1 << 32)
NEG_INF_HI16 = KEY_NEG_INF >> 16


def _glu_kernel(x_ref, w_ref, b_ref, u_ref):
    d = u_ref.shape[-1]
    xb = x_ref[...].astype(BF16)
    a = _mm(xb, w_ref[:, :d]) + b_ref[:, :d]
    g = _mm(xb, w_ref[:, d:]) + b_ref[:, d:]
    u_ref[...] = a * jax.nn.sigmoid(g)


def glu(x, w_bf, b):
    m, d = x.shape
    tm = _row_tile(m)
    return pl.pallas_call(
        _glu_kernel,
        grid=(m // tm,),
        in_specs=[pl.BlockSpec((tm, d), lambda i: (i, 0)), _resident((d, 2 * d)), _resident((1, 2 * d))],
        out_specs=pl.BlockSpec((tm, d), lambda i: (i, 0)),
        out_shape=jax.ShapeDtypeStruct((m, d), F32),
        compiler_params=_params(1),
        name="glu",
    )(x, w_bf, b.reshape(1, 2 * d))


CONV_HALO = 32
CONV_ROWS = 64
CONV_SEQS = 8


def _conv_prompt_kernel(u_ref, halo_ref, wdw_ref, bdw_ref, g_ref, b_ref, z_ref, win_ref, y_ref):
    i = pl.program_id(0)
    tm = u_ref.shape[0]
    ncb = win_ref.shape[0]
    halo = jnp.where(i > 0, halo_ref[...], 0.0)
    for cb in range(ncb):
        win_ref[cb, 0:CONV_HALO, :] = halo[:, cb * LANES:(cb + 1) * LANES]
        win_ref[cb, CONV_HALO:CONV_HALO + tm, :] = u_ref[:, cb * LANES:(cb + 1) * LANES]
    shift = CONV_HALO - CONV_STATE

    def cb_body(cb, carry):
        w = wdw_ref[cb]
        for r in range(tm // CONV_ROWS):
            acc = jnp.zeros((CONV_ROWS, LANES), F32)
            for j in range(CONV_WIDTH):
                acc = acc + win_ref[cb, pl.ds(r * CONV_ROWS + shift + j, CONV_ROWS), :] * w[j:j + 1, :]
            y_ref[cb, r * CONV_ROWS:(r + 1) * CONV_ROWS, :] = acc
        return carry

    lax.fori_loop(0, ncb, cb_body, 0)
    y = jnp.concatenate([y_ref[cb] for cb in range(ncb)], axis=1) + bdw_ref[...]
    z = _layer_norm(y, g_ref[...], b_ref[...])
    z_ref[...] = (z * jax.nn.sigmoid(z)).astype(BF16)


def conv_prompt(u, w_dw, b_dw, g_n, b_n):
    m, d = u.shape
    tm = _row_tile(m)
    ncb = d // LANES
    wdw = jnp.pad(w_dw, ((0, 1), (0, 0))).reshape(CONV_WIDTH + 1, ncb, LANES).transpose(1, 0, 2)
    hb = tm // CONV_HALO
    return pl.pallas_call(
        _conv_prompt_kernel,
        grid=(m // tm,),
        in_specs=[pl.BlockSpec((tm, d), lambda i: (i, 0)),
                  pl.BlockSpec((CONV_HALO, d), lambda i: (jnp.maximum(i * hb - 1, 0), 0)),
                  _resident((ncb, CONV_WIDTH + 1, LANES)),
                  _resident((1, d)), _resident((1, d)), _resident((1, d))],
        out_specs=pl.BlockSpec((tm, d), lambda i: (i, 0)),
        out_shape=jax.ShapeDtypeStruct((m, d), BF16),
        scratch_shapes=[pltpu.VMEM((ncb, tm + CONV_HALO, LANES), F32), pltpu.VMEM((ncb, tm, LANES), F32)],
        compiler_params=_params(1),
        name="conv_prompt",
    )(u, u, wdw, b_dw.reshape(1, d), g_n.reshape(1, d), b_n.reshape(1, d))


def _conv_sample_kernel(st_ref, u_ref, wdw_ref, bdw_ref, g_ref, b_ref, z_ref, ns_ref, win_ref):
    nb, t, _ = u_ref.shape
    for s in range(nb):
        win_ref[0:CONV_STATE, :] = st_ref[s]
        win_ref[CONV_STATE:CONV_STATE + t, :] = u_ref[s]
        acc = jnp.zeros(u_ref.shape[1:], F32)
        for j in range(CONV_WIDTH):
            acc = acc + win_ref[j:j + t, :] * wdw_ref[j:j + 1, :]
        z = _layer_norm(acc + bdw_ref[...], g_ref[...], b_ref[...])
        z_ref[s] = z * jax.nn.sigmoid(z)
        ns_ref[s] = win_ref[t:t + CONV_STATE, :]


def conv_sample(u, state, w_dw, b_dw, g_n, b_n):
    b, t, d = u.shape
    nb = math.gcd(b, CONV_SEQS)
    return pl.pallas_call(
        _conv_sample_kernel,
        grid=(b // nb,),
        in_specs=[pl.BlockSpec((nb, CONV_STATE, d), lambda i: (i, 0, 0)),
                  pl.BlockSpec((nb, t, d), lambda i: (i, 0, 0)),
                  _resident((CONV_WIDTH + 1, d)), _resident((1, d)), _resident((1, d)), _resident((1, d))],
        out_specs=[pl.BlockSpec((nb, t, d), lambda i: (i, 0, 0)),
                   pl.BlockSpec((nb, CONV_STATE, d), lambda i: (i, 0, 0))],
        out_shape=[jax.ShapeDtypeStruct((b, t, d), F32), jax.ShapeDtypeStruct((b, CONV_STATE, d), F32)],
        scratch_shapes=[pltpu.VMEM((CONV_STATE + t + 2, d), F32)],
        compiler_params=_params(1),
        name="conv_sample",
    )(state, u, jnp.pad(w_dw, ((0, 1), (0, 0))), b_dw.reshape(1, d), g_n.reshape(1, d), b_n.reshape(1, d))


def _proj_res_ln_kernel(z_ref, w_ref, bias_ref, x_ref, g_ref, b_ref, o_ref):
    m = _mm(z_ref[...].astype(BF16), w_ref[...]) + bias_ref[...]
    o_ref[...] = _layer_norm(ALPHA * x_ref[...] + m, g_ref[...], b_ref[...])


def proj_res_ln(z, w_bf, bias, x, g, b):
    m, d = x.shape
    k = z.shape[1]
    tm = _row_tile(m)
    return pl.pallas_call(
        _proj_res_ln_kernel,
        grid=(m // tm,),
        in_specs=[pl.BlockSpec((tm, k), lambda i: (i, 0)), _resident((k, d)), _resident((1, d)),
                  pl.BlockSpec((tm, d), lambda i: (i, 0)), _resident((1, d)), _resident((1, d))],
        out_specs=pl.BlockSpec((tm, d), lambda i: (i, 0)),
        out_shape=jax.ShapeDtypeStruct((m, d), F32),
        compiler_params=_params(1),
        name="proj_res_ln",
    )(z, w_bf, bias.reshape(1, d), x, g.reshape(1, d), b.reshape(1, d))


FFN_CHUNK = 256


def _ffn_kernel(x_ref, win_ref, wout_ref, g_ref, b_ref, o_ref):
    x = x_ref[...]
    xb = x.astype(BF16)
    dff = wout_ref.shape[0]
    acc = jnp.zeros(x.shape, F32)
    for c in range(dff // FFN_CHUNK):
        lo = c * FFN_CHUNK
        gate = _mm(xb, win_ref[:, lo:lo + FFN_CHUNK])
        up = _mm(xb, win_ref[:, dff + lo:dff + lo + FFN_CHUNK])
        act = (gate * jax.nn.sigmoid(gate) * up).astype(BF16)
        acc = acc + _mm(act, wout_ref[lo:lo + FFN_CHUNK, :])
    o_ref[...] = _layer_norm(ALPHA * x + acc, g_ref[...], b_ref[...])


def ffn(x, win_bf, wout_bf, g, b):
    m, d = x.shape
    dff = wout_bf.shape[0]
    tm = _row_tile(m)
    return pl.pallas_call(
        _ffn_kernel,
        grid=(m // tm,),
        in_specs=[pl.BlockSpec((tm, d), lambda i: (i, 0)), _resident((d, 2 * dff)), _resident((dff, d)),
                  _resident((1, d)), _resident((1, d))],
        out_specs=pl.BlockSpec((tm, d), lambda i: (i, 0)),
        out_shape=jax.ShapeDtypeStruct((m, d), F32),
        compiler_params=_params(1),
        name="ffn",
    )(x, win_bf, wout_bf, g.reshape(1, d), b.reshape(1, d))


Q_SCALE = HEAD_DIM ** -0.5 * LOG2E
W_SCALE = IDX_HEADS ** -0.5 * IDX_DIM ** -0.5
QI_PAD = IDX_HEADS * LANES
V_ROWS = HEAD_DIM + 16
VT_ROWS = N_HEADS * V_ROWS


def _split_attn_weights(w_in):
    d = D_MODEL
    o3 = 3 * d
    o4 = o3 + IDX_HEADS * IDX_DIM
    o5 = o4 + IDX_DIM
    wq, wk, wv = w_in[:, :d], w_in[:, d:2 * d], w_in[:, 2 * d:o3]
    wqi = w_in[:, o3:o4].reshape(d, IDX_HEADS, IDX_DIM)
    wqi = jnp.pad(wqi, ((0, 0), (0, 0), (0, LANES - IDX_DIM))).reshape(d, QI_PAD)
    wkw = jnp.pad(w_in[:, o4:], ((0, 0), (0, LANES - IDX_DIM - IDX_HEADS)))
    return tuple(a.astype(BF16) for a in (wq, wk, wv, wqi, wkw))


def _attn_proj_t_kernel(x_ref, wk_ref, wv_ref, wkw_ref, wqT_ref, wvT_ref, wqiT_ref, wwT_ref,
                        k32_ref, v32_ref, kiw_ref, qT_ref, kbf_ref, vT_ref, qiT_ref, kip_ref, wT_ref):
    xb = x_ref[...].astype(BF16)
    k = _mm(xb, wk_ref[...])
    k32_ref[...] = k
    kbf_ref[...] = k.astype(BF16)
    v32_ref[...] = _mm(xb, wv_ref[...])
    kiw = _mm(xb, wkw_ref[...])
    kiw_ref[...] = kiw
    lane = lax.broadcasted_iota(I32, kiw.shape, 1)
    kip_ref[...] = jnp.where(lane < IDX_DIM, kiw, 0.0).astype(BF16)
    qT_ref[...] = (_mm_nt(wqT_ref[...], xb) * Q_SCALE).astype(BF16)
    vt = _mm_nt(wvT_ref[...], xb).astype(BF16)
    ones = jnp.ones((V_ROWS - HEAD_DIM, vt.shape[1]), BF16)
    for h in range(N_HEADS):
        vT_ref[h * V_ROWS:h * V_ROWS + HEAD_DIM, :] = vt[h * HEAD_DIM:(h + 1) * HEAD_DIM]
        vT_ref[h * V_ROWS + HEAD_DIM:(h + 1) * V_ROWS, :] = ones
    qiT_ref[...] = _mm_nt(wqiT_ref[...], xb).astype(BF16)
    wT_ref[...] = _mm_nt(wwT_ref[...], xb) * W_SCALE


def attn_proj_t(x, w_in):
    m, d = x.shape
    tm = _row_tile(m)
    wq, wk, wv, wqi, wkw = _split_attn_weights(w_in)
    o5 = 3 * d + IDX_HEADS * IDX_DIM + IDX_DIM
    wwT = jnp.pad(w_in[:, o5:].T, ((0, 16 - IDX_HEADS), (0, 0))).astype(BF16)
    row = lambda n: pl.BlockSpec((tm, n), lambda i: (i, 0))
    col = lambda n: pl.BlockSpec((n, tm), lambda i: (0, i))
    return pl.pallas_call(
        _attn_proj_t_kernel,
        grid=(m // tm,),
        in_specs=[row(d), _resident((d, d)), _resident((d, d)), _resident((d, LANES)),
                  _resident((d, d)), _resident((d, d)), _resident((QI_PAD, d)), _resident((16, d))],
        out_specs=[row(d), row(d), row(LANES), col(d), row(d), col(VT_ROWS), col(QI_PAD), row(LANES), col(16)],
        out_shape=[jax.ShapeDtypeStruct((m, d), F32), jax.ShapeDtypeStruct((m, d), F32),
                   jax.ShapeDtypeStruct((m, LANES), F32), jax.ShapeDtypeStruct((d, m), BF16),
                   jax.ShapeDtypeStruct((m, d), BF16), jax.ShapeDtypeStruct((VT_ROWS, m), BF16),
                   jax.ShapeDtypeStruct((QI_PAD, m), BF16), jax.ShapeDtypeStruct((m, LANES), BF16),
                   jax.ShapeDtypeStruct((16, m), F32)],
        compiler_params=_params(1),
        name="attn_proj_t",
    )(x, wk, wv, wkw, wq.T, wv.T, wqi.T, wwT)


def _attn_proj_s_kernel(x_ref, wq_ref, wk_ref, wv_ref, wqi_ref, wkw_ref,
                        q_ref, k32_ref, v32_ref, qi_ref, kiw_ref):
    xb = x_ref[...].astype(BF16)
    q_ref[...] = (_mm(xb, wq_ref[...]) * Q_SCALE).astype(BF16)
    k32_ref[...] = _mm(xb, wk_ref[...])
    v32_ref[...] = _mm(xb, wv_ref[...])
    qi_ref[...] = _mm(xb, wqi_ref[...]).astype(BF16)
    kiw_ref[...] = _mm(xb, wkw_ref[...])


def attn_proj_s(x, w_in):
    m, d = x.shape
    tm = _row_tile(m)
    wq, wk, wv, wqi, wkw = _split_attn_weights(w_in)
    row = lambda n: pl.BlockSpec((tm, n), lambda i: (i, 0))
    return pl.pallas_call(
        _attn_proj_s_kernel,
        grid=(m // tm,),
        in_specs=[row(d), _resident((d, d)), _resident((d, d)), _resident((d, d)),
                  _resident((d, QI_PAD)), _resident((d, LANES))],
        out_specs=[row(d), row(d), row(d), row(QI_PAD), row(LANES)],
        out_shape=[jax.ShapeDtypeStruct((m, d), BF16), jax.ShapeDtypeStruct((m, d), F32),
                   jax.ShapeDtypeStruct((m, d), F32), jax.ShapeDtypeStruct((m, QI_PAD), BF16),
                   jax.ShapeDtypeStruct((m, LANES), F32)],
        compiler_params=_params(1),
        name="attn_proj_s",
    )(x, wq, wk, wv, wqi, wkw)


def _prompt_schedule(t):
    qb_of, kt_of, fl = [], [], []
    for qb in range(t // BQ):
        nb = -(-max(qb * BQ - CH, 0) // TK)
        qb_of.append(qb); kt_of.append(0); fl.append(1 | (2 if nb == 0 else 0))
        for kt in range(nb):
            qb_of.append(qb); kt_of.append(kt); fl.append(2 if kt == nb - 1 else 0)
    return (np.asarray(qb_of, np.int32), np.asarray(kt_of, np.int32), np.asarray(fl, np.int32))


def _attn_prompt_kernel(qb_ref, kt_ref, fl_ref,
                        qT_ref, qiT_ref, wT_ref, kip_ref, kb_ref, vTb_ref, kp_ref, kd_ref, vTp_ref, vTd_ref,
                        rb_ref, o_ref,
                        s_ref, h_ref, thr_ref, bb_ref, m_ref, l_ref, acc_ref, madd_ref, *, k_sel, idx_bits):
    step = pl.program_id(0)
    qb = qb_ref[step]
    kt = kt_ref[step]
    flags = fl_ref[step]
    is_band = (flags & 1) != 0
    is_last = (flags & 2) != 0
    q0 = qb * BQ
    nch = 2 * (qb + 1)
    cband = 2 * qb - 1
    kf = float(k_sel)

    @pl.when(step == 0)
    def _build_band_bias():
        r = lax.broadcasted_iota(I32, (BAND, BQ), 0)
        j = lax.broadcasted_iota(I32, (BAND, BQ), 1)
        dist = CH + j - r
        for h in range(N_HEADS):
            bb_ref[h] = _bias_from_dist(dist, rb_ref, h)

    def head_cols(h):
        return slice(h * HEAD_DIM, (h + 1) * HEAD_DIM)

    def head_vrows(h):
        return slice(h * V_ROWS, (h + 1) * V_ROWS)

    def qk(h, kh):
        n = kh.shape[0] // QK_ROWS
        return jnp.concatenate([_mm(kh[i * QK_ROWS:(i + 1) * QK_ROWS], qT_ref[head_cols(h), :])
                                for i in range(n)], axis=0)

    def softmax_pv(h, lt, vth, madd, bias, cbias):
        hs = head_cols(h)
        if bias is not None:
            lt = lt + bias
        lt = lt + madd
        mx = jnp.max(lt, axis=0, keepdims=True) + cbias
        m_old = m_ref[h:h + 1, :]
        m_new = jnp.maximum(m_old, mx)
        alpha = jnp.exp2(m_old - m_new)
        pb = jnp.exp2(lt - (m_new - cbias)).astype(BF16)
        r = pb.shape[0]
        if r % (2 * PV_KEYS) == 0:
            pv = _mm(vth[:, :r // 2], pb[:r // 2]) + _mm(vth[:, r // 2:], pb[r // 2:])
        else:
            pv = _mm(vth, pb)
        acc_ref[hs, :] = alpha * acc_ref[hs, :] + pv[0:HEAD_DIM]
        l_ref[h:h + 1, :] = alpha * l_ref[h:h + 1, :] + pv[HEAD_DIM:HEAD_DIM + 1]
        m_ref[h:h + 1, :] = m_new

    def attend_heads(keys_of, vals_of, madd, bias_of, cbias_of):
        lt = qk(0, keys_of(0))
        for h in range(N_HEADS):
            lt_next = qk(h + 1, keys_of(h + 1)) if h + 1 < N_HEADS else None
            softmax_pv(h, lt, vals_of(h), madd, bias_of(h), cbias_of(h))
            lt = lt_next

    @pl.when(is_band)
    def _band_step():
        def idx_body(gi, carry):
            rows = IDX_GROUP * CH
            kic = kip_ref[pl.ds(pl.multiple_of(gi * rows, rows), rows), :]
            acc = None
            for h in range(IDX_HEADS):
                sc = _mm(kic, qiT_ref[h * LANES:(h + 1) * LANES, :])
                val = jnp.maximum(sc, 0.0) * wT_ref[h:h + 1, :]
                acc = val if acc is None else acc + val
            acc = jnp.where(acc == 0.0, 0.0, acc)
            hi = _key_hi16(acc)
            for i in range(IDX_GROUP):
                s_ref[gi * IDX_GROUP + i] = acc[i * CH:(i + 1) * CH]
                h_ref[gi * IDX_GROUP + i] = hi[i * CH:(i + 1) * CH]
            return carry

        ngrp = (nch + IDX_GROUP - 1) // IDX_GROUP
        lax.fori_loop(0, ngrp, idx_body, 0)
        rr = lax.broadcasted_iota(I32, (CH, BQ), 0)
        jj = lax.broadcasted_iota(I32, (CH, BQ), 1)
        for i in range(BQ // CH):
            c = 2 * qb + i
            masked = jnp.where(i * CH + rr <= jj, s_ref[c], NEG_INF)
            s_ref[c] = masked
            h_ref[c] = _key_hi16(masked)

        @pl.when(ngrp * IDX_GROUP > nch)
        def _mask_group_tail():
            for c in range(IDX_GROUP - BQ // CH):
                s_ref[nch + c] = jnp.full((CH, BQ), NEG_INF, F32)
                h_ref[nch + c] = jnp.full((CH, BQ), NEG_INF_HI16, jnp.int16)

        def count(pred):
            def body(g, acc):
                for c in range(IDX_GROUP):
                    ci = g * IDX_GROUP + c
                    ind = jnp.where(pred(ci, s_ref[ci]), 1.0, 0.0)
                    for r in range(CH // SUBLANES):
                        acc = acc + ind[r * SUBLANES:(r + 1) * SUBLANES]
                return acc
            acc = lax.fori_loop(0, ngrp, body, jnp.zeros((SUBLANES, BQ), F32))
            return jnp.broadcast_to(jnp.sum(acc, axis=0, keepdims=True), (SUBLANES, BQ))

        def count_hi(pred):
            one, zero = jnp.ones((), jnp.int16), jnp.zeros((), jnp.int16)

            def body(g, acc):
                for c in range(IDX_GROUP):
                    ind = jnp.where(pred(h_ref[g * IDX_GROUP + c]), one, zero)
                    for r in range(CH // PACKED_ROWS):
                        acc = acc + ind[r * PACKED_ROWS:(r + 1) * PACKED_ROWS]
                return acc
            acc = lax.fori_loop(0, ngrp, body, jnp.zeros((PACKED_ROWS, BQ), jnp.int16))
            tot = jnp.sum(acc.astype(I32).astype(F32), axis=0, keepdims=True)
            return jnp.broadcast_to(tot, (SUBLANES, BQ))

        def row16(v):
            return jnp.broadcast_to(v[0:1, :], (PACKED_ROWS, BQ)).astype(jnp.int16)[0:1, :]

        def search16(cnt_min, target, early_exit):
            c0 = count_hi(lambda x: x >= jnp.zeros((), jnp.int16))
            ok0 = jnp.logical_and(c0 >= target, cnt_min != target)
            ans = jnp.where(ok0, 0, I16_MIN).astype(I32)
            cnt = jnp.where(ok0, c0, cnt_min)

            def bit_step(i, ans, cnt):
                cand = ans + jnp.left_shift(jnp.int32(1), 14 - i)
                c16 = row16(cand)
                c = count_hi(lambda x: x >= c16)
                ok = jnp.logical_and(c >= target, cnt != target)
                return jnp.where(ok, cand, ans), jnp.where(ok, c, cnt)

            if not early_exit:
                return lax.fori_loop(0, 15, lambda i, c: bit_step(i, *c), (ans, cnt))

            ans, cnt = bit_step(jnp.int32(0), ans, cnt)

            def cond(carry):
                j, _, cnt = carry
                return jnp.logical_and(j < 7, jnp.max(jnp.abs(cnt - target)) > 0.5)

            def body(carry):
                j, ans, cnt = carry
                ans, cnt = bit_step(2 * j + 1, ans, cnt)
                ans, cnt = bit_step(2 * j + 2, ans, cnt)
                return j + 1, ans, cnt

            _, ans, cnt = lax.while_loop(cond, body, (jnp.int32(0), ans, cnt))
            return ans, cnt

        total = (ngrp * (IDX_GROUP * CH)).astype(F32)
        k_vec = jnp.full((SUBLANES, BQ), kf, F32)
        ans_hi, cnt_hi = search16(jnp.broadcast_to(total, (SUBLANES, BQ)), k_vec, early_exit=False)
        p16 = row16(ans_hi)
        n_above = count_hi(lambda x: x > p16)
        def lo_body(g, carry):
            for c in range(IDX_GROUP):
                ci = g * IDX_GROUP + c
                h_ref[ci] = jnp.where(h_ref[ci] == p16, _key_lo16(s_ref[ci]), jnp.int16(I16_MIN))
            return carry

        lax.fori_loop(0, ngrp, lo_body, 0)
        ans_lo, cnt_lo = search16(cnt_hi - n_above, k_vec - n_above, early_exit=True)
        ans = ans_hi * (2 ** 16) + (ans_lo - I16_MIN)
        cnt = n_above + cnt_lo
        none = ans <= KEY_NEG_INF
        thr = jnp.where(none, NEG_INF, _key_to_f32(ans))
        thr_ref[...] = thr

        has = jnp.logical_and(cnt > kf, jnp.logical_not(none))
        any_ties = jnp.max(jnp.where(has, 1.0, 0.0)) > 0.5

        @pl.when(any_ties)
        def _resolve_ties():
            t1 = thr[0:1, :]
            n_gt = count(lambda c, x: x > t1)
            need = jnp.where(has, kf - n_gt, 3.0e38)
            rows = lax.broadcasted_iota(I32, (CH, BQ), 0)

            def j_body(i, j):
                cj = j + jnp.left_shift(jnp.int32(1), idx_bits - 1 - i)
                c1 = cj[0:1, :]
                f = count(lambda c, x: jnp.logical_and(x == t1, c * CH + rows < c1))
                return jnp.where(f < need, cj, j)

            jstar = lax.fori_loop(0, idx_bits, j_body, jnp.zeros((SUBLANES, BQ), I32))
            j1 = jstar[0:1, :]
            h1 = has[0:1, :]

            def demote(c, carry):
                x = s_ref[c]
                drop = jnp.logical_and(jnp.logical_and(x == t1, c * CH + rows > j1), h1)
                s_ref[c] = jnp.where(drop, NEG_INF, x)
                return carry

            lax.fori_loop(0, nch, demote, 0)

        m_ref[...] = jnp.full(m_ref.shape, M_FLOOR, F32)
        l_ref[...] = jnp.zeros(l_ref.shape, F32)
        acc_ref[...] = jnp.zeros(acc_ref.shape, F32)
        thr1 = thr_ref[0:1, :]
        cprev = jnp.maximum(cband, 0)
        sc = jnp.concatenate([s_ref[cprev], s_ref[2 * qb], s_ref[2 * qb + 1]], axis=0)
        r = lax.broadcasted_iota(I32, (BAND, BQ), 0)
        j = lax.broadcasted_iota(I32, (BAND, BQ), 1)
        ok = jnp.logical_and(sc >= thr1, r - CH <= j)
        ok = jnp.logical_and(ok, jnp.logical_or(r >= CH, qb > 0))
        madd = jnp.where(ok, 0.0, NEG_INF)
        attend_heads(lambda h: jnp.concatenate([kp_ref[:, head_cols(h)], kd_ref[:, head_cols(h)]], axis=0),
                     lambda h: jnp.concatenate([vTp_ref[head_vrows(h), :], vTd_ref[head_vrows(h), :]], axis=1),
                     madd, lambda h: bb_ref[h], lambda h: 0.0)

    @pl.when(jnp.logical_not(is_band))
    def _bulk_step():
        thr1 = thr_ref[0:1, :]
        for i in range(TK // CH):
            c = kt * (TK // CH) + i
            x = s_ref[jnp.minimum(c, cband - 1)]
            sel = jnp.where(x >= thr1, 0.0, NEG_INF)
            madd_ref[i * CH:(i + 1) * CH, :] = jnp.where(c < cband, sel, NEG_INF)
        attend_heads(lambda h: kb_ref[:, head_cols(h)], lambda h: vTb_ref[head_vrows(h), :], madd_ref[...],
                     lambda h: None, lambda h: rb_ref[REL_BUCKETS - 1, h] * LOG2E)

    @pl.when(is_last)
    def _finalize():
        for h in range(N_HEADS):
            hs = slice(h * HEAD_DIM, (h + 1) * HEAD_DIM)
            lsum = l_ref[h:h + 1, :]
            o_ref[:, hs] = (acc_ref[hs, :] / lsum).T.astype(BF16)


def attn_prompt(qT, qiT, wT, kip, kbf, vT, rel_bias):
    d, t = qT.shape
    k_sel = min(TOPK_MAX, t // 4)
    qb_of, kt_of, fl = _prompt_schedule(t)
    n_steps = len(qb_of)
    tk = min(TK, t)
    qcol = lambda n: pl.BlockSpec((n, BQ), lambda s, qb, kt, fl: (0, qb[s]))
    in_specs = [
        qcol(d), qcol(QI_PAD), qcol(16),
        pl.BlockSpec((t, LANES), lambda s, qb, kt, fl: (0, 0), pipeline_mode=pl.Buffered(1)),
        pl.BlockSpec((tk, d), lambda s, qb, kt, fl: (kt[s], 0)),
        pl.BlockSpec((VT_ROWS, tk), lambda s, qb, kt, fl: (0, kt[s])),
        pl.BlockSpec((CH, d), lambda s, qb, kt, fl: (jnp.maximum(2 * qb[s] - 1, 0), 0)),
        pl.BlockSpec((BQ, d), lambda s, qb, kt, fl: (qb[s], 0)),
        pl.BlockSpec((VT_ROWS, CH), lambda s, qb, kt, fl: (0, jnp.maximum(2 * qb[s] - 1, 0))),
        pl.BlockSpec((VT_ROWS, BQ), lambda s, qb, kt, fl: (0, qb[s])),
        pl.BlockSpec(memory_space=pltpu.SMEM),
    ]
    grid_spec = pltpu.PrefetchScalarGridSpec(
        num_scalar_prefetch=3,
        grid=(n_steps,),
        in_specs=in_specs,
        out_specs=pl.BlockSpec((BQ, d), lambda s, qb, kt, fl: (qb[s], 0)),
        scratch_shapes=[
            pltpu.VMEM((t // CH, CH, BQ), F32),
            pltpu.VMEM((t // CH, CH, BQ), jnp.int16),
            pltpu.VMEM((SUBLANES, BQ), F32),
            pltpu.VMEM((N_HEADS, BAND, BQ), F32),
            pltpu.VMEM((N_HEADS, BQ), F32),
            pltpu.VMEM((N_HEADS, BQ), F32),
            pltpu.VMEM((d, BQ), F32),
            pltpu.VMEM((tk, BQ), F32),
        ],
    )
    kern = functools.partial(_attn_prompt_kernel, k_sel=k_sel, idx_bits=max(t.bit_length() - 1, 1))
    return pl.pallas_call(
        kern,
        grid_spec=grid_spec,
        out_shape=jax.ShapeDtypeStruct((t, d), BF16),
        compiler_params=_params(1),
        name="attn_prompt",
    )(jnp.asarray(qb_of), jnp.asarray(kt_of), jnp.asarray(fl),
      qT, qiT, wT, kip, kbf, vT, kbf, kbf, vT, vT, rel_bias)


def _sample_scores_kernel(pt_ref, qi_ref, w_ref, *rest, n_groups):
    pages = rest[:S1_PAGES]
    kin_ref, o_ref, kbuf_ref = rest[S1_PAGES:]
    b = pl.program_id(0)
    g = pl.program_id(1)
    t = o_ref.shape[1]

    @pl.when(jnp.logical_and(b == 0, g == 0))
    def _zero_pad_rows():
        kbuf_ref[...] = jnp.zeros(kbuf_ref.shape, BF16)

    def head_sum(val):
        s = val[0:t]
        for h in range(1, IDX_HEADS):
            s = s + val[h * t:(h + 1) * t]
        return s

    wcol = w_ref[:, 0:1]

    @pl.when(g < n_groups)
    def _past_pages():
        for i in range(S1_PAGES):
            kbuf_ref[0:IDX_DIM, i * PAGE_SIZE:(i + 1) * PAGE_SIZE] = pages[i][...].astype(BF16)
        s = head_sum(jnp.maximum(_mm(qi_ref[...], kbuf_ref[...]), 0.0) * wcol)
        for i in range(S1_PAGES):
            o_ref[i] = s[:, i * PAGE_SIZE:(i + 1) * PAGE_SIZE]

    @pl.when(g == n_groups)
    def _new_keys():
        s = head_sum(jnp.maximum(_mm_nt(qi_ref[...], kin_ref[...]), 0.0) * wcol)
        row = lax.broadcasted_iota(I32, s.shape, 0)
        lane = lax.broadcasted_iota(I32, s.shape, 1)
        o_ref[0] = jnp.where(lane <= row, s, NEG_INF)
        for i in range(1, S1_PAGES):
            o_ref[i] = jnp.full(s.shape, NEG_INF, F32)


def sample_scores(page_table, qi_ht, w_ht, cache_kidx, layer, ki_new):
    b, n_pages = page_table.shape
    t = qi_ht.shape[1] // IDX_HEADS
    n_groups = n_pages // S1_PAGES
    page_spec = lambda i: pl.BlockSpec(
        (None, None, IDX_DIM, PAGE_SIZE),
        lambda bi, g, pt: (layer, pt[bi, jnp.minimum(g, n_groups - 1) * S1_PAGES + i], 0, 0))
    grid_spec = pltpu.PrefetchScalarGridSpec(
        num_scalar_prefetch=1,
        grid=(b, n_groups + 1),
        in_specs=[pl.BlockSpec((None, IDX_HEADS * t, LANES), lambda bi, g, pt: (bi, 0, 0)),
                  pl.BlockSpec((None, IDX_HEADS * t, LANES), lambda bi, g, pt: (bi, 0, 0))]
                 + [page_spec(i) for i in range(S1_PAGES)]
                 + [pl.BlockSpec((None, PAGE_SIZE, LANES), lambda bi, g, pt: (bi, 0, 0))],
        out_specs=pl.BlockSpec((None, S1_PAGES, t, LANES), lambda bi, g, pt: (bi, g, 0, 0)),
        scratch_shapes=[pltpu.VMEM((LANES, S1_PAGES * PAGE_SIZE), BF16)],
    )
    kidx_t = jnp.swapaxes(cache_kidx, 2, 3)
    return pl.pallas_call(
        functools.partial(_sample_scores_kernel, n_groups=n_groups),
        grid_spec=grid_spec,
        out_shape=jax.ShapeDtypeStruct((b, (n_groups + 1) * S1_PAGES, t, LANES), F32),
        compiler_params=_params(2),
        name="sample_scores",
    )(page_table, qi_ht, w_ht, *([kidx_t] * S1_PAGES), ki_new)


def _sample_select_kernel(s_ref, o_ref, *, k_sel, past, idx_bits):
    x = s_ref[...]
    nb, nck, t, _ = x.shape
    kf = float(k_sel)
    cidx = lax.broadcasted_iota(I32, x.shape, 1)
    row = lax.broadcasted_iota(I32, x.shape, 2)
    lane = lax.broadcasted_iota(I32, x.shape, 3)
    idx = cidx * LANES + lane

    def count(pred):
        per_lane = jnp.sum(jnp.where(pred, 1.0, 0.0), axis=1)
        return jnp.broadcast_to(jnp.sum(per_lane, axis=2, keepdims=True), (nb, t, LANES))

    def count_ge(cand_key):
        return count(x >= _key_to_f32(cand_key)[:, None])

    total = float(nck * LANES)
    c0 = count(x >= 0.0)
    ans = jnp.where(c0 >= kf, 0, INT_MIN).astype(I32)
    cnt = jnp.where(c0 >= kf, c0, total)

    def bit_cond(carry):
        i, _, cnt = carry
        return jnp.logical_and(i < 31, jnp.max(jnp.abs(cnt - kf)) > 0.5)

    def bit_body(carry):
        i, ans, cnt = carry
        cand = ans + jnp.left_shift(jnp.int32(1), 30 - i)
        c = count_ge(cand)
        ok = jnp.logical_and(c >= kf, cnt != kf)
        return i + 1, jnp.where(ok, cand, ans), jnp.where(ok, c, cnt)

    _, ans, cnt = lax.while_loop(bit_cond, bit_body, (jnp.int32(0), ans, cnt))
    none = ans == INT_MIN
    thr = jnp.where(none, NEG_INF, _key_to_f32(ans))
    has = jnp.logical_and(cnt > kf, jnp.logical_not(none))
    any_ties = jnp.max(jnp.where(has, 1.0, 0.0)) > 0.5
    new = idx - past
    valid = jnp.logical_or(idx < past, jnp.logical_and(new < t, new <= row))
    keep = jnp.logical_and(x >= thr[:, None], valid)

    @pl.when(jnp.logical_not(any_ties))
    def _no_ties():
        o_ref[...] = jnp.where(keep, 0.0, NEG_INF)

    @pl.when(any_ties)
    def _resolve_ties():
        n_gt = count(x > thr[:, None])
        need = jnp.where(has, kf - n_gt, 3.0e38)
        eq = x == thr[:, None]

        def j_body(i, j):
            cj = j + jnp.left_shift(jnp.int32(1), idx_bits - 1 - i)
            f = count(jnp.logical_and(eq, idx < cj[:, None]))
            return jnp.where(f < need, cj, j)

        jstar = lax.fori_loop(0, idx_bits, j_body, jnp.zeros((nb, t, LANES), I32))
        drop = jnp.logical_and(jnp.logical_and(eq, idx > jstar[:, None]), has[:, None])
        o_ref[...] = jnp.where(jnp.logical_and(keep, jnp.logical_not(drop)), 0.0, NEG_INF)


def sample_select(scores, past, k_sel):
    b, nck, t, _ = scores.shape
    nb = math.gcd(b, SELECT_SEQS)
    spec = pl.BlockSpec((nb, nck, t, LANES), lambda i: (i, 0, 0, 0))
    kern = functools.partial(_sample_select_kernel, k_sel=k_sel, past=past,
                             idx_bits=(nck * LANES - 1).bit_length())
    return pl.pallas_call(
        kern, grid=(b // nb,), in_specs=[spec], out_specs=spec,
        out_shape=jax.ShapeDtypeStruct(scores.shape, F32),
        compiler_params=_params(1),
        name="sample_select",
    )(scores)


def _sample_attn_kernel(pt_ref, q_ref, madd_ref, *rest, n_groups, t):
    kpages = rest[:S3_PAGES]
    vpages = rest[S3_PAGES:2 * S3_PAGES]
    (kn_ref, vn_ref, rb_ref, o_ref,
     kcat_ref, vcat_ref, m_ref, l_ref, acc_ref, bconst_ref, blast_ref, bnew_ref) = rest[2 * S3_PAGES:]
    b = pl.program_id(0)
    g = pl.program_id(1)
    rows = N_HEADS * t

    @pl.when(jnp.logical_and(b == 0, g == 0))
    def _build_bias_tables():
        tq = lax.broadcasted_iota(I32, (t, LANES), 0)
        ln = lax.broadcasted_iota(I32, (t, LANES), 1)
        for h in range(N_HEADS):
            hs = slice(h * t, (h + 1) * t)
            bconst_ref[hs, :] = jnp.full((t, LANES), rb_ref[REL_BUCKETS - 1, h] * LOG2E, F32)
            blast_ref[hs, :] = _bias_from_dist(PAGE_SIZE + tq - ln, rb_ref, h)
            bnew_ref[hs, :] = _bias_from_dist(tq - ln, rb_ref, h)

    @pl.when(g == 0)
    def _init():
        m_ref[...] = jnp.full(m_ref.shape, M_FLOOR, F32)
        l_ref[...] = jnp.zeros(l_ref.shape, F32)
        acc_ref[...] = jnp.zeros(acc_ref.shape, F32)

    def tile_heads(m8):
        return jnp.concatenate([m8] * N_HEADS, axis=0)

    def attend(kk, vv, bias, madd):
        lg = _mm_nt(q_ref[...], kk) + bias + madd
        mx = jnp.max(lg, axis=1, keepdims=True)
        m_old = m_ref[...]
        m_new = jnp.maximum(m_old, mx)
        alpha = jnp.exp2(m_old - m_new)
        p = jnp.exp2(lg - m_new[:, 0:1])
        l_ref[...] = alpha * l_ref[...] + jnp.sum(p, axis=1, keepdims=True)
        acc_ref[...] = alpha[:, 0:1] * acc_ref[...] + _mm(p.astype(BF16), vv)
        m_ref[...] = m_new

    @pl.when(g < n_groups)
    def _cached_pages():
        for i in range(S3_PAGES):
            rs = slice(i * PAGE_SIZE, (i + 1) * PAGE_SIZE)
            for h in range(N_HEADS):
                hs = slice(h * HEAD_DIM, (h + 1) * HEAD_DIM)
                head_rows = pl.ds(h, PAGE_SIZE, stride=N_HEADS)
                kcat_ref[rs, hs] = kpages[i][head_rows, :].astype(BF16)
                vcat_ref[rs, hs] = vpages[i][head_rows, :].astype(BF16)
        last = jnp.where(g == n_groups - 1, blast_ref[...], bconst_ref[...])
        bias = jnp.concatenate([bconst_ref[...]] * (S3_PAGES - 1) + [last], axis=1)
        madd = jnp.concatenate([tile_heads(madd_ref[i]) for i in range(S3_PAGES)], axis=1)
        attend(kcat_ref[...], vcat_ref[...], bias, madd)

    @pl.when(g == n_groups)
    def _new_keys_and_finish():
        attend(kn_ref[...], vn_ref[...], bnew_ref[...], tile_heads(madd_ref[0]))
        for h in range(N_HEADS):
            hs = slice(h * HEAD_DIM, (h + 1) * HEAD_DIM)
            rs = slice(h * t, (h + 1) * t)
            o_ref[:, hs] = acc_ref[rs, hs] / l_ref[rs, 0:1]


def sample_attn(page_table, q_bd, madd, cache_k, cache_v, layer, k_new, v_new, rel_bias):
    b, n_pages = page_table.shape
    rows, d = q_bd.shape[1:]
    t = rows // N_HEADS
    n_groups = n_pages // S3_PAGES
    page_spec = lambda i: pl.BlockSpec(
        (None, None, PAGE_SIZE * N_HEADS, HEAD_DIM),
        lambda bi, g, pt: (layer, pt[bi, jnp.minimum(g, n_groups - 1) * S3_PAGES + i], 0, 0))
    per_b = lambda shape: pl.BlockSpec((None,) + shape, lambda bi, g, pt: (bi,) + (0,) * len(shape))
    grid_spec = pltpu.PrefetchScalarGridSpec(
        num_scalar_prefetch=1,
        grid=(b, n_groups + 1),
        in_specs=[per_b((rows, d)),
                  pl.BlockSpec((None, S3_PAGES, t, LANES), lambda bi, g, pt: (bi, g, 0, 0))]
                 + [page_spec(i) for i in range(S3_PAGES)] * 2
                 + [per_b((PAGE_SIZE, d)), per_b((PAGE_SIZE, d)), pl.BlockSpec(memory_space=pltpu.SMEM)],
        out_specs=per_b((t, d)),
        scratch_shapes=[pltpu.VMEM((S3_PAGES * PAGE_SIZE, d), BF16), pltpu.VMEM((S3_PAGES * PAGE_SIZE, d), BF16),
                        pltpu.VMEM((rows, LANES), F32), pltpu.VMEM((rows, LANES), F32),
                        pltpu.VMEM((rows, d), F32),
                        pltpu.VMEM((rows, LANES), F32), pltpu.VMEM((rows, LANES), F32),
                        pltpu.VMEM((rows, LANES), F32)],
    )
    return pl.pallas_call(
        functools.partial(_sample_attn_kernel, n_groups=n_groups, t=t),
        grid_spec=grid_spec,
        out_shape=jax.ShapeDtypeStruct((b, t, d), F32),
        compiler_params=_params(2),
        name="sample_attn",
    )(page_table, q_bd, madd, *([cache_k] * S3_PAGES), *([cache_v] * S3_PAGES), k_new, v_new, rel_bias)


def attn_sample(x, cache_k, cache_v, cache_kidx, page_table, layer, w_in, rel_bias):
    b, t, d = x.shape
    past = page_table.shape[1] * PAGE_SIZE
    k_sel = min(TOPK_MAX, (past + t) // 4)
    q, k32, v32, qi, kiw = attn_proj_s(x.reshape(b * t, d), w_in)
    qi_ht = qi.reshape(b, t, IDX_HEADS, LANES).transpose(0, 2, 1, 3).reshape(b, IDX_HEADS * t, LANES)
    wi = kiw[:, IDX_DIM:IDX_DIM + IDX_HEADS] * W_SCALE
    w_ht = jnp.broadcast_to(wi.reshape(b, t, IDX_HEADS).transpose(0, 2, 1).reshape(b, IDX_HEADS * t, 1),
                            (b, IDX_HEADS * t, LANES))
    head_of_col = jnp.arange(d, dtype=I32) // HEAD_DIM
    q_bd = jnp.where(head_of_col[None, None, None, :] == jnp.arange(N_HEADS, dtype=I32)[None, :, None, None],
                     q.reshape(b, 1, t, d), jnp.zeros((), BF16)).reshape(b, N_HEADS * t, d)
    pad_rows = lambda a: jnp.pad(a.reshape(b, t, -1), ((0, 0), (0, PAGE_SIZE - t), (0, 0))).astype(BF16)
    ki_new = pad_rows(jnp.pad(kiw[:, :IDX_DIM], ((0, 0), (0, LANES - IDX_DIM))))
    scores = sample_scores(page_table, qi_ht, w_ht, cache_kidx, layer, ki_new)
    madd = sample_select(scores, past, k_sel)
    page_rows = lambda c: c.reshape(c.shape[0], c.shape[1], PAGE_SIZE * N_HEADS, HEAD_DIM)
    out = sample_attn(page_table, q_bd, madd, page_rows(cache_k), page_rows(cache_v), layer,
                      pad_rows(k32), pad_rows(v32), rel_bias)
    return out.reshape(b * t, d), k32, v32, kiw[:, :IDX_DIM]


def kernel(x_prompt, x_sample, state_conv, cache_k, cache_v, cache_kidx, page_table, rel_bias,
           w_pw1, b_pw1, w_dw, b_dw, conv_norm_g, conv_norm_b, w_pw2, b_pw2,
           w_attn_in, w_attn_out, w_ffn_in, w_ffn_out, ln_mix_g, ln_mix_b, ln_ffn_g, ln_ffn_b):
    bp, tp, d = x_prompt.shape
    bs, ts, _ = x_sample.shape
    assert bp == 1 and d == D_MODEL and tp % TK == 0
    xp = x_prompt.reshape(tp, d)
    xs = x_sample.reshape(bs * ts, d)
    bf = lambda a: a.astype(BF16)

    w1, w2 = bf(w_pw1[0]), bf(w_pw2[0])
    up = glu(xp, w1, b_pw1[0])
    us = glu(xs, w1, b_pw1[0])
    zp = conv_prompt(up, w_dw[0], b_dw[0], conv_norm_g[0], conv_norm_b[0])
    zs, conv_s = conv_sample(us.reshape(bs, ts, d), state_conv[0], w_dw[0], b_dw[0],
                             conv_norm_g[0], conv_norm_b[0])
    conv_p = up[tp - CONV_STATE:]
    xp = proj_res_ln(zp, w2, b_pw2[0], xp, ln_mix_g[0], ln_mix_b[0])
    xs = proj_res_ln(zs.reshape(bs * ts, d), w2, b_pw2[0], xs, ln_mix_g[0], ln_mix_b[0])
    wf_in, wf_out = bf(w_ffn_in[0]), bf(w_ffn_out[0])
    xp = ffn(xp, wf_in, wf_out, ln_ffn_g[0], ln_ffn_b[0])
    xs = ffn(xs, wf_in, wf_out, ln_ffn_g[0], ln_ffn_b[0])

    w_o = bf(w_attn_out[0])
    no_bias = jnp.zeros((d,), F32)
    kp32, vp32, kiwp, qT, kbf, vT, qiT, kip, wT = attn_proj_t(xp, w_attn_in[0])
    ap = attn_prompt(qT, qiT, wT, kip, kbf, vT, rel_bias)
    xp = proj_res_ln(ap, w_o, no_bias, xp, ln_mix_g[1], ln_mix_b[1])
    a_s, ks32, vs32, kis = attn_sample(xs.reshape(bs, ts, d), cache_k, cache_v, cache_kidx, page_table, 0,
                                       w_attn_in[0], rel_bias)
    xs = proj_res_ln(a_s, w_o, no_bias, xs, ln_mix_g[1], ln_mix_b[1])
    wf_in, wf_out = bf(w_ffn_in[1]), bf(w_ffn_out[1])
    xp = ffn(xp, wf_in, wf_out, ln_ffn_g[1], ln_ffn_b[1])
    xs = ffn(xs, wf_in, wf_out, ln_ffn_g[1], ln_ffn_b[1])

    return (xp.reshape(1, tp, d), xs.reshape(bs, ts, d),
            conv_p.reshape(1, 1, CONV_STATE, d), conv_s.reshape(1, bs, CONV_STATE, d),
            kp32.reshape(1, 1, tp, N_HEADS, HEAD_DIM), vp32.reshape(1, 1, tp, N_HEADS, HEAD_DIM),
            kiwp[:, :IDX_DIM].reshape(1, 1, tp, IDX_DIM),
            ks32.reshape(1, bs, ts, N_HEADS, HEAD_DIM), vs32.reshape(1, bs, ts, N_HEADS, HEAD_DIM),
            kis.reshape(1, bs, ts, IDX_DIM))
```
